```python
import math
import jax, jax.numpy as jnp
from jax import lax
import numpy as np

D_MODEL = 4096
BATCH = 2
SEQ = 8192
DEPTH = 1

A_HEADS = 32
A_KV_HEADS = 4
A_HEAD_DIM = 64
A_WINDOW = 128
B_HEADS = 16
B_KV_GROUPS = 2
B_HEAD_DIM = 128
CMP_BLOCK = 32
CMP_STRIDE = 16
CMP_HIDDEN = 128
SLC_BLOCK = 64
N_SELECT = 16
B_WINDOW = 512
BAND_BLOCK = 128
SLC_QBLOCK = 64
D_FF = 11008
PLE_DIM = 256
ROPE_THETA = 10000.0
EPS = 1e-6

COLS = [
    A_HEADS * A_HEAD_DIM,
    A_KV_HEADS * A_HEAD_DIM,
    A_KV_HEADS * A_HEAD_DIM,
    B_HEADS * B_HEAD_DIM,
    B_KV_GROUPS * B_HEAD_DIM,
    B_KV_GROUPS * B_HEAD_DIM,
    B_KV_GROUPS * B_HEAD_DIM,
    B_KV_GROUPS * B_HEAD_DIM,
    B_KV_GROUPS * B_HEAD_DIM,
    B_KV_GROUPS * B_HEAD_DIM,
    B_HEADS * 3,
    D_MODEL,
    D_MODEL,
]
D_IN = sum(COLS)
SPLITS = [int(c) for c in np.cumsum(COLS)[:-1]]

kernel_name = "hybrid_swa_sink_nsa_macaron_ple"


def rms_norm(x, g):
    xf = x.astype(jnp.float32)
    var = jnp.mean(xf * xf, axis=-1, keepdims=True)
    return (xf * lax.rsqrt(var + EPS) * g.astype(jnp.float32)).astype(x.dtype)


def swiglu(x, w_gate, w_up, w_down):
    return (jax.nn.silu(x @ w_gate) * (x @ w_up)) @ w_down


def rope(x, positions):
    d = x.shape[-1]
    inv = jnp.power(jnp.float32(ROPE_THETA), -jnp.arange(0, d, 2, dtype=jnp.float32) / d)
    ang = positions.astype(jnp.float32)[:, :, None] * inv
    cos = jnp.cos(ang)[:, :, None, :]
    sin = jnp.sin(ang)[:, :, None, :]
    xf = x.astype(jnp.float32)
    x1, x2 = xf[..., : d // 2], xf[..., d // 2:]
    return jnp.concatenate([x1 * cos - x2 * sin, x2 * cos + x1 * sin], axis=-1).astype(x.dtype)


def banded_attention(q, k, v, window, sinks=None):
    B, T, H, D = q.shape
    G = k.shape[2]
    R = H // G
    blk = BAND_BLOCK
    nb = T // blk
    P = -(-(window - 1) // blk)
    qb = q.reshape(B, nb, blk, G, R, D)
    pad = ((0, 0), (P * blk, 0), (0, 0), (0, 0))
    kb = jnp.pad(k, pad).reshape(B, nb + P, blk, G, D)
    vb = jnp.pad(v, pad).reshape(B, nb + P, blk, G, D)
    kw = jnp.concatenate([kb[:, i:i + nb] for i in range(P + 1)], axis=2)
    vw = jnp.concatenate([vb[:, i:i + nb] for i in range(P + 1)], axis=2)
    s = jnp.einsum('bnqgrd,bnkgd->bngrqk', qb, kw).astype(jnp.float32) * (D ** -0.5)
    kwin = (P + 1) * blk
    qpos = jnp.arange(blk)[:, None] + P * blk
    kpos = jnp.arange(kwin)[None, :]
    diff = qpos - kpos
    mask_rel = (diff >= 0) & (diff < window)
    mask_abs = (jnp.arange(nb)[:, None] * blk + kpos - P * blk) >= 0
    mask = mask_rel[None, :, :] & mask_abs[:, None, :]
    s = jnp.where(mask[None, :, None, None], s, -jnp.inf)
    if sinks is not None:
        sk = sinks.astype(jnp.float32).reshape(1, 1, G, R, 1, 1)
        m = jnp.maximum(jnp.max(s, axis=-1, keepdims=True), sk)
        e = jnp.exp(s - m)
        prob = e / (jnp.sum(e, axis=-1, keepdims=True) + jnp.exp(sk - m))
    else:
        prob = jax.nn.softmax(s, axis=-1)
    o = jnp.einsum('bngrqk,bnkgd->bnqgrd', prob.astype(v.dtype), vw)
    return o.reshape(B, T, H, D)


def compress(kv, pe, w1, w2):
    B, T, G, D = kv.shape
    nc = (T - CMP_BLOCK) // CMP_STRIDE + 1
    idx = jnp.arange(nc)[:, None] * CMP_STRIDE + jnp.arange(CMP_BLOCK)[None, :]
    blocks = kv[:, idx] + pe[None, None, :, None, :]
    flat = jnp.moveaxis(blocks, 3, 2).reshape(B, nc, G, CMP_BLOCK * D)
    return jax.nn.gelu(flat @ w1) @ w2


def selection_map(nc, ns):
    a, b = SLC_BLOCK // CMP_STRIDE, CMP_BLOCK // CMP_STRIDE
    j = np.arange(ns)[:, None, None]
    c = a * j - np.arange(a)[None, :, None] - np.arange(b)[None, None, :]
    jj = np.broadcast_to(j, c.shape)
    ok = (c >= 0) & (c < nc)
    mat = np.zeros((nc, ns), np.float32)
    np.add.at(mat, (c[ok], jj[ok]), 1.0)
    return jnp.asarray(mat)


def selected_attention(q, k, v, sel):
    B, T, H, D = q.shape
    G = k.shape[2]
    R = H // G
    n = sel.shape[-1]
    ns = T // SLC_BLOCK
    qbs = SLC_QBLOCK
    nqb = T // qbs
    kbt = k.reshape(B, ns, SLC_BLOCK, G, D).transpose(0, 3, 1, 2, 4)
    vbt = v.reshape(B, ns, SLC_BLOCK, G, D).transpose(0, 3, 1, 2, 4)
    bi = jnp.arange(B)[:, None, None, None]
    gi = jnp.arange(G)[None, :, None, None]
    q_blocks = q.reshape(B, nqb, qbs, G, R, D).transpose(1, 0, 2, 3, 4, 5)
    sel_blocks = sel.reshape(B, G, nqb, qbs, n).transpose(2, 0, 1, 3, 4)
    starts = jnp.arange(nqb, dtype=jnp.int32) * qbs
    scale = D ** -0.5

    def one_block(args):
        qblk, sblk, t0 = args
        kg = kbt[bi, gi, sblk]
        vg = vbt[bi, gi, sblk]
        s = jnp.einsum('bqgrd,bgqnld->bgrqnl', qblk, kg).astype(jnp.float32) * scale
        kpos = sblk[..., None] * SLC_BLOCK + jnp.arange(SLC_BLOCK)
        tpos = t0 + jnp.arange(qbs)
        mask = kpos <= tpos[None, None, :, None, None]
        s = jnp.where(mask[:, :, None], s, -jnp.inf)
        prob = jax.nn.softmax(s.reshape(B, G, R, qbs, n * SLC_BLOCK), axis=-1).reshape(s.shape)
        return jnp.einsum('bgrqnl,bgqnld->bqgrd', prob.astype(v.dtype), vg)

    o = lax.map(one_block, (q_blocks, sel_blocks, starts))
    return o.transpose(1, 0, 2, 3, 4, 5).reshape(B, T, H, D)


def nsa_attention(q, kc, vc, ks, vs, kw, vw, gates, positions,
                  pe_k, w_ck1, w_ck2, pe_v, w_cv1, w_cv2):
    B, T, H, D = q.shape
    G = kc.shape[2]
    R = H // G
    scale = D ** -0.5
    kcmp = compress(kc, pe_k, w_ck1, w_ck2)
    vcmp = compress(vc, pe_v, w_cv1, w_cv2)
    nc = kcmp.shape[1]
    qg = q.reshape(B, T, G, R, D)
    s = jnp.einsum('btgrd,bcgd->bgrtc', qg, kcmp).astype(jnp.float32) * scale
    t_idx = jnp.arange(T)
    visible = (jnp.arange(nc) * CMP_STRIDE + CMP_BLOCK - 1)[None, :] <= t_idx[:, None]
    s = jnp.where(visible, s, -jnp.inf)
    m = jnp.max(s, axis=-1, keepdims=True)
    m = jnp.where(jnp.isfinite(m), m, 0.0)
    e = jnp.exp(s - m)
    den = jnp.sum(e, axis=-1, keepdims=True)
    p_cmp = e / jnp.where(den > 0, den, 1.0)
    o_cmp = jnp.einsum('bgrtc,bcgd->btgrd', p_cmp.astype(vcmp.dtype), vcmp).reshape(B, T, H, D)
    ns = T // SLC_BLOCK
    n_sel = min(N_SELECT, ns)
    imp = jnp.einsum('bgrtc,cj->bgtj', p_cmp, selection_map(nc, ns))
    j = jnp.arange(ns)[None, :]
    cur = (t_idx // SLC_BLOCK)[:, None]
    valid = j * SLC_BLOCK <= t_idx[:, None]
    forced = (j == 0) | (j == cur) | (j == cur - 1)
    score = jnp.where(forced, jnp.inf, jnp.where(valid, imp, -jnp.inf))
    _, sel = lax.top_k(score, n_sel)
    qr = rope(q, positions)
    o_slc = selected_attention(qr, rope(ks, positions), vs, sel)
    o_win = banded_attention(qr, rope(kw, positions), vw, B_WINDOW)
    g = jax.nn.sigmoid(gates.astype(jnp.float32)).astype(q.dtype)
    return g[..., 0:1] * o_cmp + g[..., 1:2] * o_slc + g[..., 2:3] * o_win


def setup_inputs(seed: int = 0) -> dict:
    key = jax.random.key(seed)
    ks = jax.random.split(key, 32)

    def w(k, shape, fan_in):
        return jax.random.normal(k, shape, jnp.float32) * (fan_in ** -0.5)

    def gain(k, shape):
        return 1.0 + 0.02 * jax.random.normal(k, shape, jnp.float32)

    L = DEPTH
    x = jax.random.normal(ks[0], (BATCH, SEQ, D_MODEL), jnp.float32)
    p = jax.random.normal(ks[1], (DEPTH, BATCH, SEQ, PLE_DIM), jnp.float32)
    offs = jax.random.randint(ks[2], (BATCH, 1), 0, 4096, dtype=jnp.int32)
    positions = (jnp.arange(SEQ, dtype=jnp.int32)[None, :] + offs).astype(jnp.int32)
    return {
        "x": x,
        "p": p,
        "positions": positions,
        "ffn1_norm": gain(ks[3], (L, D_MODEL)),
        "ffn1_w_gate": w(ks[4], (L, D_MODEL, D_FF), D_MODEL),
        "ffn1_w_up": w(ks[5], (L, D_MODEL, D_FF), D_MODEL),
        "ffn1_w_down": w(ks[6], (L, D_FF, D_MODEL), D_FF),
        "mix_norm": gain(ks[7], (L, D_MODEL)),
        "w_in": w(ks[8], (L, D_MODEL, D_IN), D_MODEL),
        "sinks": jax.random.normal(ks[9], (L, A_HEADS), jnp.float32),
        "nsa_pe_k": 0.1 * jax.random.normal(ks[10], (L, CMP_BLOCK, B_HEAD_DIM), jnp.float32),
        "nsa_w_ck1": w(ks[11], (L, CMP_BLOCK * B_HEAD_DIM, CMP_HIDDEN), CMP_BLOCK * B_HEAD_DIM),
        "nsa_w_ck2": w(ks[12], (L, CMP_HIDDEN, B_HEAD_DIM), CMP_HIDDEN),
        "nsa_pe_v": 0.1 * jax.random.normal(ks[13], (L, CMP_BLOCK, B_HEAD_DIM), jnp.float32),
        "nsa_w_cv1": w(ks[14], (L, CMP_BLOCK * B_HEAD_DIM, CMP_HIDDEN), CMP_BLOCK * B_HEAD_DIM),
        "nsa_w_cv2": w(ks[15], (L, CMP_HIDDEN, B_HEAD_DIM), CMP_HIDDEN),
        "w_o_a": w(ks[16], (L, A_HEADS * A_HEAD_DIM, D_MODEL), A_HEADS * A_HEAD_DIM),
        "w_o_b": w(ks[17], (L, B_HEADS * B_HEAD_DIM, D_MODEL), B_HEADS * B_HEAD_DIM),
        "w_o": w(ks[18], (L, D_MODEL, D_MODEL), D_MODEL),
        "ffn2_norm": gain(ks[19], (L, D_MODEL)),
        "ffn2_w_gate": w(ks[20], (L, D_MODEL, D_FF), D_MODEL),
        "ffn2_w_up": w(ks[21], (L, D_MODEL, D_FF), D_MODEL),
        "ffn2_w_down": w(ks[22], (L, D_FF, D_MODEL), D_FF),
        "ple_norm": gain(ks[23], (L, D_MODEL)),
        "w_ple_gate": w(ks[24], (L, D_MODEL, D_MODEL), D_MODEL),
        "w_ple_proj": w(ks[25], (L, PLE_DIM, D_MODEL), PLE_DIM),
        "final_norm": gain(ks[26], (D_MODEL,)),
    }


def reference(x, p, positions, ffn1_norm, ffn1_w_gate, ffn1_w_up, ffn1_w_down,
              mix_norm, w_in, sinks, nsa_pe_k, nsa_w_ck1, nsa_w_ck2,
              nsa_pe_v, nsa_w_cv1, nsa_w_cv2, w_o_a, w_o_b, w_o,
              ffn2_norm, ffn2_w_gate, ffn2_w_up, ffn2_w_down,
              ple_norm, w_ple_gate, w_ple_proj, final_norm):
    B, T, _ = x.shape
    h = x
    for i in range(DEPTH):
        h = h + 0.5 * swiglu(rms_norm(h, ffn1_norm[i]), ffn1_w_gate[i], ffn1_w_up[i], ffn1_w_down[i])
        u = rms_norm(h, mix_norm[i])
        (qa, ka, va, qb, kc, vc, ksl, vsl, kwn, vwn, gb, gate_a, gate_b) = jnp.split(
            u @ w_in[i], SPLITS, axis=-1)
        qa = rope(qa.reshape(B, T, A_HEADS, A_HEAD_DIM), positions)
        ka = rope(ka.reshape(B, T, A_KV_HEADS, A_HEAD_DIM), positions)
        va = va.reshape(B, T, A_KV_HEADS, A_HEAD_DIM)
        o_a = banded_attention(qa, ka, va, A_WINDOW, sinks[i])
        kv_shape = (B, T, B_KV_GROUPS, B_HEAD_DIM)
        o_b = nsa_attention(
            qb.reshape(B, T, B_HEADS, B_HEAD_DIM),
            kc.reshape(kv_shape), vc.reshape(kv_shape),
            ksl.reshape(kv_shape), vsl.reshape(kv_shape),
            kwn.reshape(kv_shape), vwn.reshape(kv_shape),
            gb.reshape(B, T, B_HEADS, 3), positions,
            nsa_pe_k[i], nsa_w_ck1[i], nsa_w_ck2[i], nsa_pe_v[i], nsa_w_cv1[i], nsa_w_cv2[i])
        y_a = o_a.reshape(B, T, -1) @ w_o_a[i]
        y_b = o_b.reshape(B, T, -1) @ w_o_b[i]
        merged = jax.nn.sigmoid(gate_a) * y_a + jax.nn.sigmoid(gate_b) * y_b
        h = h + merged @ w_o[i]
        h = h + 0.5 * swiglu(rms_norm(h, ffn2_norm[i]), ffn2_w_gate[i], ffn2_w_up[i], ffn2_w_down[i])
        ple_gate = jax.nn.sigmoid(rms_norm(h, ple_norm[i]) @ w_ple_gate[i])
        h = h + (p[i] @ w_ple_proj[i]) * ple_gate
    return rms_norm(h, final_norm)
```

```python
import functools

import numpy as np
import jax
import jax.numpy as jnp
from jax import lax
from jax.experimental import pallas as pl
from jax.experimental.pallas import tpu as pltpu

f32 = jnp.float32
bf16 = jnp.bfloat16

A_HEADS, A_KV_HEADS, A_HEAD_DIM, A_WINDOW = 32, 4, 64, 128
B_HEADS, B_KV_GROUPS, B_HEAD_DIM = 16, 2, 128
CMP_BLOCK, CMP_STRIDE = 32, 16
SLC_BLOCK, N_SELECT, B_WINDOW = 64, 16, 512
BAND_BLOCK = 128
ROPE_THETA = 10000.0
EPS = 1e-6

LANES = 128
VMEM_LIMIT = 56 * 1024 * 1024
MASKED = -1e30
M_INIT = -1e29

_NT = (((1,), (1,)), ((), ()))


def _params(sem):
    return pltpu.CompilerParams(dimension_semantics=sem, vmem_limit_bytes=VMEM_LIMIT)


def _rmsnorm(x, g):
    var = jnp.mean(x * x, axis=-1, keepdims=True)
    return x * lax.rsqrt(var + EPS) * g


def _sigmoid(x):
    return 1.0 / (1.0 + jnp.exp(-x))


def _ffn_kernel(x_ref, g_ref, wg_ref, wu_ref, wd_ref, o_ref, xn_ref):
    @pl.when(pl.program_id(1) == 0)
    def _():
        x = x_ref[...]
        xn_ref[...] = _rmsnorm(x, g_ref[...]).astype(bf16)
        o_ref[...] = x

    xn = xn_ref[...]
    gate = jnp.dot(xn, wg_ref[...], preferred_element_type=f32)
    up = jnp.dot(xn, wu_ref[...], preferred_element_type=f32)
    act = (0.5 * gate * _sigmoid(gate) * up).astype(bf16)
    o_ref[...] += jnp.dot(act, wd_ref[...], preferred_element_type=f32)


def _ffn(x, g, wg, wu, wd, *, tm, tf):
    n, d = x.shape
    ff = wg.shape[1]
    return pl.pallas_call(
        _ffn_kernel,
        out_shape=jax.ShapeDtypeStruct((n, d), f32),
        grid=(n // tm, ff // tf),
        in_specs=[
            pl.BlockSpec((tm, d), lambda i, j: (i, 0), pipeline_mode=pl.Buffered(1)),
            pl.BlockSpec((1, d), lambda i, j: (0, 0)),
            pl.BlockSpec((d, tf), lambda i, j: (0, j)),
            pl.BlockSpec((d, tf), lambda i, j: (0, j)),
            pl.BlockSpec((tf, d), lambda i, j: (j, 0)),
        ],
        out_specs=pl.BlockSpec((tm, d), lambda i, j: (i, 0)),
        scratch_shapes=[pltpu.VMEM((tm, d), bf16)],
        compiler_params=_params(("parallel", "arbitrary")),
        name="ffn",
    )(x, g.reshape(1, d), wg, wu, wd)


def _norm_cast_kernel(x_ref, g_ref, o_ref):
    o_ref[...] = _rmsnorm(x_ref[...], g_ref[...]).astype(bf16)


def _norm_cast(x, g, *, tm):
    n, d = x.shape
    return pl.pallas_call(
        _norm_cast_kernel,
        out_shape=jax.ShapeDtypeStruct((n, d), bf16),
        grid=(n // tm,),
        in_specs=[pl.BlockSpec((tm, d), lambda i: (i, 0)),
                  pl.BlockSpec((1, d), lambda i: (0, 0))],
        out_specs=pl.BlockSpec((tm, d), lambda i: (i, 0)),
        compiler_params=_params(("parallel",)),
        name="norm_cast",
    )(x, g.reshape(1, d))


def _rope_table_kernel(pos_ref, inv64_ref, inv128_ref, c64_ref, s64_ref, c128_ref, s128_ref):
    pos = pos_ref[...].astype(f32)
    lane = lax.broadcasted_iota(jnp.int32, c64_ref.shape, 1)
    a64 = pos * inv64_ref[...]
    a128 = pos * inv128_ref[...]
    c64_ref[...] = jnp.cos(a64)
    s64_ref[...] = jnp.where(lane % A_HEAD_DIM < A_HEAD_DIM // 2, -1.0, 1.0) * jnp.sin(a64)
    c128_ref[...] = jnp.cos(a128)
    s128_ref[...] = jnp.where(lane < B_HEAD_DIM // 2, -1.0, 1.0) * jnp.sin(a128)


def _rope_tables(positions, *, tm):
    n = positions.size

    def inv(d):
        v = jnp.power(jnp.float32(ROPE_THETA), -jnp.arange(0, d, 2, dtype=f32) / d)
        return jnp.tile(v, LANES // (d // 2)).reshape(1, LANES)

    tab = jax.ShapeDtypeStruct((n, LANES), f32)
    row = pl.BlockSpec((1, LANES), lambda i: (0, 0))
    blk = pl.BlockSpec((tm, LANES), lambda i: (i, 0))
    return pl.pallas_call(
        _rope_table_kernel,
        out_shape=(tab, tab, tab, tab),
        grid=(n // tm,),
        in_specs=[pl.BlockSpec((tm, 1), lambda i: (i, 0)), row, row],
        out_specs=(blk, blk, blk, blk),
        compiler_params=_params(("parallel",)),
        name="rope_tables",
    )(positions.reshape(n, 1), inv(A_HEAD_DIM), inv(B_HEAD_DIM))


def _proj_plain_kernel(x_ref, w_ref, o_ref):
    o_ref[...] = jnp.dot(x_ref[...], w_ref[...], preferred_element_type=f32).astype(o_ref.dtype)


def _proj_rope64_kernel(x_ref, w_ref, c_ref, s_ref, o_ref):
    acc = jnp.dot(x_ref[...], w_ref[...], preferred_element_type=f32)
    tn = acc.shape[1]
    half = A_HEAD_DIM // 2
    first = lax.broadcasted_iota(jnp.int32, c_ref.shape, 1) % A_HEAD_DIM < half
    cos, sin = c_ref[...], s_ref[...]
    for c in range(tn // LANES):
        seg = acc[:, c * LANES:(c + 1) * LANES]
        rot = jnp.where(first, pltpu.roll(seg, LANES - half, 1), pltpu.roll(seg, half, 1))
        o_ref[:, c * LANES:(c + 1) * LANES] = (seg * cos + rot * sin).astype(o_ref.dtype)


def _proj_rope128_kernel(x_ref, w_ref, c_ref, s_ref, o_rot_ref, o_raw_ref, *, n_scaled_tiles, scale):
    acc = jnp.dot(x_ref[...], w_ref[...], preferred_element_type=f32)
    acc = acc * jnp.where(pl.program_id(1) < n_scaled_tiles, scale, 1.0).astype(f32)
    tn = acc.shape[1]
    cos, sin = c_ref[...], s_ref[...]
    o_raw_ref[...] = acc.astype(o_raw_ref.dtype)
    for c in range(tn // LANES):
        seg = acc[:, c * LANES:(c + 1) * LANES]
        rot = pltpu.roll(seg, B_HEAD_DIM // 2, 1)
        o_rot_ref[:, c * LANES:(c + 1) * LANES] = (seg * cos + rot * sin).astype(o_rot_ref.dtype)


def _proj(xn, w, *, tm, tn, mode, tables=None, n_scaled_tiles=0, scale=1.0):
    n, d = xn.shape
    width = w.shape[1]
    x_spec = pl.BlockSpec((tm, d), lambda i, j: (i, 0))
    w_spec = pl.BlockSpec((d, tn), lambda i, j: (0, j))
    o_spec = pl.BlockSpec((tm, tn), lambda i, j: (i, j))
    t_spec = pl.BlockSpec((tm, LANES), lambda i, j: (i, 0))
    out = jax.ShapeDtypeStruct((n, width), bf16)
    common = dict(grid=(n // tm, width // tn),
                  compiler_params=_params(("parallel", "arbitrary")))
    if mode == "plain":
        return pl.pallas_call(_proj_plain_kernel, out_shape=out, in_specs=[x_spec, w_spec],
                              out_specs=o_spec, name="proj_plain", **common)(xn, w)
    if mode == "rope64":
        return pl.pallas_call(_proj_rope64_kernel, out_shape=out,
                              in_specs=[x_spec, w_spec, t_spec, t_spec],
                              out_specs=o_spec, name="proj_rope64", **common)(xn, w, *tables)
    kern = functools.partial(_proj_rope128_kernel, n_scaled_tiles=n_scaled_tiles, scale=scale)
    return pl.pallas_call(kern, out_shape=(out, out),
                          in_specs=[x_spec, w_spec, t_spec, t_spec],
                          out_specs=(o_spec, o_spec), name="proj_rope128", **common)(xn, w, *tables)


def _banded_kernel(sink_ref, q_ref, k_ref, v_ref, o_ref, *, groups, rep, dim, window, prev,
                   tq, use_sink, q_scale):
    q0 = pl.program_id(1) * tq
    slab = prev + tq
    start = pl.multiple_of(jnp.maximum(q0 - prev, 0), tq)
    tpos = q0 + lax.broadcasted_iota(jnp.int32, (tq, slab), 0)
    kpos = start + lax.broadcasted_iota(jnp.int32, (tq, slab), 1)
    diff = tpos - kpos
    mask = ((diff >= 0) & (diff < window))[None]
    for g in range(groups):
        heads = [g * rep + r for r in range(rep)]
        q = jnp.concatenate([q_ref[:, h * dim:(h + 1) * dim] for h in heads], axis=0)
        if q_scale != 1.0:
            q = q * q_scale
        ks = k_ref[pl.ds(start, slab), g * dim:(g + 1) * dim]
        vs = v_ref[pl.ds(start, slab), g * dim:(g + 1) * dim]
        s = lax.dot_general(q, ks, _NT, preferred_element_type=f32).reshape(rep, tq, slab)
        s = jnp.where(mask, s, -jnp.inf)
        m = jnp.max(s, axis=-1, keepdims=True)
        if use_sink:
            sk = jnp.concatenate([jnp.full((1, tq, 1), sink_ref[h], f32) for h in heads], axis=0)
            m = jnp.maximum(m, sk)
        e = jnp.exp(s - m)
        den = jnp.sum(e, axis=-1, keepdims=True)
        if use_sink:
            den = den + jnp.exp(sk - m)
        o = jnp.dot(e.reshape(rep * tq, slab).astype(bf16), vs, preferred_element_type=f32)
        o = o.reshape(rep, tq, dim) / den
        for r, h in enumerate(heads):
            o_ref[:, h * dim:(h + 1) * dim] = o[r].astype(o_ref.dtype)


def _banded(q, k, v, sinks, *, batch, seq, q_col, k_col, v_col, groups, rep, dim, window,
            use_sink, q_scale=1.0):
    tq = BAND_BLOCK
    prev = -(-(window - 1) // tq) * tq
    nq = seq // tq
    hw, gw = groups * rep * dim, groups * dim
    kern = functools.partial(_banded_kernel, groups=groups, rep=rep, dim=dim, window=window,
                             prev=prev, tq=tq, use_sink=use_sink, q_scale=q_scale)
    return pl.pallas_call(
        kern,
        out_shape=jax.ShapeDtypeStruct((batch * seq, hw), bf16),
        grid=(batch, nq),
        in_specs=[
            pl.BlockSpec(memory_space=pltpu.SMEM),
            pl.BlockSpec((tq, hw), lambda b, i: (b * nq + i, q_col)),
            pl.BlockSpec((seq, gw), lambda b, i: (b, k_col)),
            pl.BlockSpec((seq, gw), lambda b, i: (b, v_col)),
        ],
        out_specs=pl.BlockSpec((tq, hw), lambda b, i: (b * nq + i, 0)),
        compiler_params=_params(("parallel", "arbitrary")),
        name="banded_sink" if use_sink else "banded",
    )(sinks, q, k, v)


def _gelu_tanh(x):
    return x * (0.5 * (1.0 + jnp.tanh(np.sqrt(2.0 / np.pi).astype(np.float32)
                                      * (x + 0.044715 * (x * x * x)))))


def _compress_kernel(r_ref, pek_ref, wk1_ref, wk2_ref, pev_ref, wv1_ref, wv2_ref,
                     kc_ref, vc_ref, *, groups, dim, n_real):
    ncp = r_ref.shape[0]
    tok_w = 2 * groups * dim
    halves = CMP_BLOCK // CMP_STRIDE
    row = lax.broadcasted_iota(jnp.int32, (ncp, dim), 0)
    for which, (pe_ref, w1_ref, w2_ref, out_ref) in enumerate(
            ((pek_ref, wk1_ref, wk2_ref, kc_ref), (pev_ref, wv1_ref, wv2_ref, vc_ref))):
        for g in range(groups):
            h1 = None
            for half in range(halves):
                pieces = []
                for tl in range(CMP_STRIDE):
                    l = half * CMP_STRIDE + tl
                    col = tl * tok_w + which * groups * dim + g * dim
                    pieces.append((r_ref[:, col:col + dim].astype(f32) + pe_ref[l:l + 1, :]).astype(bf16))
                x = jnp.concatenate(pieces, axis=1)
                w = w1_ref[half * CMP_STRIDE * dim:(half + 1) * CMP_STRIDE * dim, :]
                part = jnp.dot(x, w, preferred_element_type=f32)
                if half:
                    part = pltpu.roll(part, ncp - half, 0)
                h1 = part if h1 is None else h1 + part
            y = _gelu_tanh(h1).astype(bf16)
            out = jnp.dot(y, w2_ref[...], preferred_element_type=f32)
            out_ref[g] = jnp.where(row < n_real, out, 0.0).astype(out_ref.dtype)


def _compress(r, pe_k, wk1, wk2, pe_v, wv1, wv2, *, batch, groups, dim, n_real):
    _, ncp, width = r.shape
    hidden = wk1.shape[1]
    full = lambda shape: pl.BlockSpec(shape, lambda b: (0,) * len(shape))
    out = jax.ShapeDtypeStruct((batch, groups, ncp, dim), bf16)
    o_spec = pl.BlockSpec((None, groups, ncp, dim), lambda b: (b, 0, 0, 0))
    kern = functools.partial(_compress_kernel, groups=groups, dim=dim, n_real=n_real)
    return pl.pallas_call(
        kern,
        out_shape=(out, out),
        grid=(batch,),
        in_specs=[pl.BlockSpec((None, ncp, width), lambda b: (b, 0, 0)),
                  full((CMP_BLOCK, dim)), full((CMP_BLOCK * dim, hidden)), full((hidden, dim)),
                  full((CMP_BLOCK, dim)), full((CMP_BLOCK * dim, hidden)), full((hidden, dim))],
        out_specs=(o_spec, o_spec),
        compiler_params=_params(("parallel",)),
        name="compress",
    )(r, pe_k, wk1, wk2, pe_v, wv1, wv2)


def _cmp_kernel(q_ref, kc_ref, vc_ref, selt_ref, o_ref, mb_ref, *, rep, dim, tq, n_sel):
    q0 = pl.program_id(2) * tq
    ncp = kc_ref.shape[0]
    nsp = selt_ref.shape[0]
    q = jnp.concatenate([q_ref[:, r * dim:(r + 1) * dim] for r in range(rep)], axis=0)
    s = lax.dot_general(q, kc_ref[...], _NT, preferred_element_type=f32).reshape(rep, tq, ncp)
    tpos = q0 + lax.broadcasted_iota(jnp.int32, (tq, ncp), 0)
    cidx = lax.broadcasted_iota(jnp.int32, (tq, ncp), 1)
    visible = (cidx * CMP_STRIDE + (CMP_BLOCK - 1) <= tpos)[None]
    s = jnp.where(visible, s, -jnp.inf)
    m = jnp.max(s, axis=-1, keepdims=True)
    m = jnp.where(m > -jnp.inf, m, 0.0)
    e = jnp.exp(s - m)
    den = jnp.sum(e, axis=-1, keepdims=True)
    p = e * (1.0 / jnp.where(den > 0, den, 1.0))
    o = jnp.dot(p.reshape(rep * tq, ncp).astype(bf16), vc_ref[...], preferred_element_type=f32)
    for r in range(rep):
        o_ref[:, r * dim:(r + 1) * dim] = o[r * tq:(r + 1) * tq].astype(o_ref.dtype)

    psum = jnp.sum(p, axis=0)
    imp = lax.dot_general(selt_ref[...], psum, _NT, preferred_element_type=f32,
                          precision=lax.Precision.HIGHEST)
    j = lax.broadcasted_iota(jnp.int32, (nsp, tq), 0)
    tt = q0 + lax.broadcasted_iota(jnp.int32, (nsp, tq), 1)
    cur = tt // SLC_BLOCK
    forced = (j == 0) | (j == cur) | (j == cur - 1)
    valid = j * SLC_BLOCK <= tt
    score = jnp.where(forced, jnp.inf, jnp.where(valid, imp, -jnp.inf))
    chosen = jnp.zeros((nsp, tq), f32)
    jf = j.astype(f32)
    for _ in range(n_sel):
        best = jnp.max(score, axis=0, keepdims=True)
        first = jnp.min(jnp.where(score == best, jf, float(nsp)), axis=0, keepdims=True)
        hit = jf == first
        chosen = jnp.where(hit, 1.0, chosen)
        score = jnp.where(hit, -jnp.inf, score)
    bias = jnp.where(chosen > 0, 0.0, MASKED)
    mb_ref[...] = bias.T.astype(mb_ref.dtype)


def _cmp_attention(q_raw, kcmp, vcmp, sel_t, *, batch, seq, groups, rep, dim, tq, n_sel):
    nq = seq // tq
    ncp = kcmp.shape[2]
    nsp = sel_t.shape[0]
    kern = functools.partial(_cmp_kernel, rep=rep, dim=dim, tq=tq, n_sel=n_sel)
    kv_spec = pl.BlockSpec((None, None, ncp, dim), lambda b, g, i: (b, g, 0, 0))
    return pl.pallas_call(
        kern,
        out_shape=(jax.ShapeDtypeStruct((batch * seq, groups * rep * dim), bf16),
                   jax.ShapeDtypeStruct((batch, groups, seq, nsp), bf16)),
        grid=(batch, groups, nq),
        in_specs=[pl.BlockSpec((tq, rep * dim), lambda b, g, i: (b * nq + i, g)),
                  kv_spec, kv_spec,
                  pl.BlockSpec((nsp, ncp), lambda b, g, i: (0, 0))],
        out_specs=(pl.BlockSpec((tq, rep * dim), lambda b, g, i: (b * nq + i, g)),
                   pl.BlockSpec((None, None, tq, nsp), lambda b, g, i: (b, g, i, 0))),
        compiler_params=_params(("parallel", "parallel", "arbitrary")),
        name="cmp_attention",
    )(q_raw, kcmp, vcmp, sel_t)


def _slc_kernel(q_ref, mb_ref, k_ref, v_ref, o_ref, kp_ref, qp_ref, *, rep, dim, tq, kt):
    i = pl.program_id(2)
    seq = k_ref.shape[0]
    nsp = mb_ref.shape[1]

    @pl.when(i == 0)
    def _():
        kp_ref[:, :dim] = k_ref[...]
        blk = lax.broadcasted_iota(jnp.int32, (seq, nsp), 0) // SLC_BLOCK
        lane = lax.broadcasted_iota(jnp.int32, (seq, nsp), 1)
        kp_ref[:, dim:] = jnp.where(blk == lane, 1.0, 0.0).astype(bf16)

    mb = mb_ref[...]
    for r in range(rep):
        qp_ref[r * tq:(r + 1) * tq, :dim] = q_ref[:, r * dim:(r + 1) * dim]
        qp_ref[r * tq:(r + 1) * tq, dim:] = mb
    qp = qp_ref[...]
    rows = rep * tq

    def scores(k0):
        return lax.dot_general(qp, kp_ref[pl.ds(k0, kt), :], _NT, preferred_element_type=f32)

    def update(carry, s, k0):
        m, l, acc = carry
        m_new = jnp.maximum(m, jnp.max(s, axis=-1, keepdims=True))
        alpha = jnp.exp(m - m_new)
        p = jnp.exp(s - m_new)
        l = alpha * l + jnp.sum(p, axis=-1, keepdims=True)
        acc = alpha * acc + jnp.dot(p.astype(bf16), v_ref[pl.ds(k0, kt), :],
                                    preferred_element_type=f32)
        return m_new, l, acc

    def body(t, carry):
        k0 = pl.multiple_of(t * kt, kt)
        return update(carry, scores(k0), k0)

    n_full = (i * tq) // kt
    init = (jnp.full((rows, 1), M_INIT, f32), jnp.zeros((rows, 1), f32), jnp.zeros((rows, dim), f32))
    carry = lax.fori_loop(0, n_full, body, init)
    k0 = pl.multiple_of(n_full * kt, kt)
    s = scores(k0).reshape(rep, tq, kt)
    tpos = i * tq + lax.broadcasted_iota(jnp.int32, (tq, kt), 0)
    kpos = k0 + lax.broadcasted_iota(jnp.int32, (tq, kt), 1)
    s = jnp.where((kpos <= tpos)[None], s, MASKED).reshape(rows, kt)
    _, l, acc = update(carry, s, k0)
    o = acc / l
    for r in range(rep):
        o_ref[:, r * dim:(r + 1) * dim] = o[r * tq:(r + 1) * tq].astype(o_ref.dtype)


def _slc_attention(q_rot, mbias, k_rot, v, *, batch, seq, groups, rep, dim, tq, kt,
                   k_col, v_col):
    nq = seq // tq
    nsp = mbias.shape[-1]
    kern = functools.partial(_slc_kernel, rep=rep, dim=dim, tq=tq, kt=kt)
    return pl.pallas_call(
        kern,
        out_shape=jax.ShapeDtypeStruct((batch * seq, groups * rep * dim), bf16),
        grid=(batch, groups, nq),
        in_specs=[pl.BlockSpec((tq, rep * dim), lambda b, g, i: (b * nq + i, g)),
                  pl.BlockSpec((None, None, tq, nsp), lambda b, g, i: (b, g, i, 0)),
                  pl.BlockSpec((seq, dim), lambda b, g, i: (b, k_col + g)),
                  pl.BlockSpec((seq, dim), lambda b, g, i: (b, v_col + g))],
        out_specs=pl.BlockSpec((tq, rep * dim), lambda b, g, i: (b * nq + i, g)),
        scratch_shapes=[pltpu.VMEM((seq, dim + nsp), bf16),
                        pltpu.VMEM((rep * tq, dim + nsp), bf16)],
        compiler_params=_params(("parallel", "parallel", "arbitrary")),
        name="slc_attention",
    )(q_rot, mbias, k_rot, v)


def _merge_kernel(oa_ref, oc_ref, os_ref, ow_ref, gb_ref, ga_ref, gbm_ref, woa_ref, wob_ref,
                  o_ref, ob_ref, *, heads, dim):
    @pl.when(pl.program_id(1) == 0)
    def _():
        g = _sigmoid(gb_ref[...].astype(f32))
        for h in range(heads):
            cols = slice(h * dim, (h + 1) * dim)
            ob = (g[:, 3 * h:3 * h + 1] * oc_ref[:, cols].astype(f32)
                  + g[:, 3 * h + 1:3 * h + 2] * os_ref[:, cols].astype(f32)
                  + g[:, 3 * h + 2:3 * h + 3] * ow_ref[:, cols].astype(f32))
            ob_ref[:, cols] = ob.astype(bf16)

    ya = jnp.dot(oa_ref[...], woa_ref[...], preferred_element_type=f32)
    yb = jnp.dot(ob_ref[...], wob_ref[...], preferred_element_type=f32)
    merged = _sigmoid(ga_ref[...].astype(f32)) * ya + _sigmoid(gbm_ref[...].astype(f32)) * yb
    o_ref[...] = merged.astype(o_ref.dtype)


def _merge(o_a, o_cmp, o_slc, o_win, proj_c, w_oa, w_ob, *, tm, tn, gb_col, ga_col, gbm_col):
    n, da = o_a.shape
    db = o_cmp.shape[1]
    d = w_oa.shape[1]
    row = lambda w: pl.BlockSpec((tm, w), lambda i, j: (i, 0))
    kern = functools.partial(_merge_kernel, heads=B_HEADS, dim=B_HEAD_DIM)
    return pl.pallas_call(
        kern,
        out_shape=jax.ShapeDtypeStruct((n, d), bf16),
        grid=(n // tm, d // tn),
        in_specs=[row(da), row(db), row(db), row(db),
                  pl.BlockSpec((tm, LANES), lambda i, j: (i, gb_col)),
                  pl.BlockSpec((tm, tn), lambda i, j: (i, ga_col + j)),
                  pl.BlockSpec((tm, tn), lambda i, j: (i, gbm_col + j)),
                  pl.BlockSpec((da, tn), lambda i, j: (0, j)),
                  pl.BlockSpec((db, tn), lambda i, j: (0, j))],
        out_specs=pl.BlockSpec((tm, tn), lambda i, j: (i, j)),
        scratch_shapes=[pltpu.VMEM((tm, db), bf16)],
        compiler_params=_params(("parallel", "arbitrary")),
        name="merge",
    )(o_a, o_cmp, o_slc, o_win, proj_c, proj_c, proj_c, w_oa, w_ob)


def _matmul_res_kernel(x_ref, w_ref, r_ref, o_ref):
    o_ref[...] = r_ref[...] + jnp.dot(x_ref[...], w_ref[...], preferred_element_type=f32)


def _matmul_res(x, w, res, *, tm, tn):
    n, k = x.shape
    d = w.shape[1]
    return pl.pallas_call(
        _matmul_res_kernel,
        out_shape=jax.ShapeDtypeStruct((n, d), f32),
        grid=(n // tm, d // tn),
        in_specs=[pl.BlockSpec((tm, k), lambda i, j: (i, 0)),
                  pl.BlockSpec((k, tn), lambda i, j: (0, j)),
                  pl.BlockSpec((tm, tn), lambda i, j: (i, j))],
        out_specs=pl.BlockSpec((tm, tn), lambda i, j: (i, j)),
        compiler_params=_params(("parallel", "arbitrary")),
        name="matmul_res",
    )(x, w, res)


def _ple_kernel(xn_ref, h_ref, p_ref, wg_ref, wp_ref, fn_ref, o_ref, h4_ref):
    j = pl.program_id(1)
    nj = pl.num_programs(1)
    tn = h_ref.shape[1]
    gate = _sigmoid(jnp.dot(xn_ref[...], wg_ref[...], preferred_element_type=f32))
    emb = jnp.dot(p_ref[...].astype(bf16), wp_ref[...], preferred_element_type=f32)
    h4_ref[j] = h_ref[...] + emb * gate

    @pl.when(j == nj - 1)
    def _():
        n_tiles = h4_ref.shape[0]
        ss = None
        for c in range(n_tiles):
            t = h4_ref[c]
            part = jnp.sum(t * t, axis=-1, keepdims=True)
            ss = part if ss is None else ss + part
        inv = lax.rsqrt(ss / (n_tiles * tn) + EPS)
        for c in range(n_tiles):
            o_ref[:, c * tn:(c + 1) * tn] = h4_ref[c] * inv * fn_ref[:, c * tn:(c + 1) * tn]


def _ple(xn, h, p, w_gate, w_proj, final_norm, *, tm, tn):
    n, d = h.shape
    pd = p.shape[1]
    return pl.pallas_call(
        _ple_kernel,
        out_shape=jax.ShapeDtypeStruct((n, d), f32),
        grid=(n // tm, d // tn),
        in_specs=[pl.BlockSpec((tm, d), lambda i, j: (i, 0)),
                  pl.BlockSpec((tm, tn), lambda i, j: (i, j)),
                  pl.BlockSpec((tm, pd), lambda i, j: (i, 0)),
                  pl.BlockSpec((d, tn), lambda i, j: (0, j)),
                  pl.BlockSpec((pd, tn), lambda i, j: (0, j)),
                  pl.BlockSpec((1, d), lambda i, j: (0, 0))],
        out_specs=pl.BlockSpec((tm, d), lambda i, j: (i, 0)),
        scratch_shapes=[pltpu.VMEM((d // tn, tm, tn), f32)],
        compiler_params=_params(("parallel", "arbitrary")),
        name="ple_final",
    )(xn, h, p, w_gate, w_proj, final_norm.reshape(1, d))


def _selection_map_t(nc, ns, ncp, nsp):
    a, b = SLC_BLOCK // CMP_STRIDE, CMP_BLOCK // CMP_STRIDE
    j = np.arange(ns)[:, None, None]
    c = a * j - np.arange(a)[None, :, None] - np.arange(b)[None, None, :]
    jj = np.broadcast_to(j, c.shape)
    ok = (c >= 0) & (c < nc)
    mat = np.zeros((nsp, ncp), np.float32)
    np.add.at(mat, (jj[ok], c[ok]), 1.0)
    return jnp.asarray(mat)


def _split_w_in(w_in):
    qa_w, kv_a = A_HEADS * A_HEAD_DIM, A_KV_HEADS * A_HEAD_DIM
    qb_w, kv_b = B_HEADS * B_HEAD_DIM, B_KV_GROUPS * B_HEAD_DIM
    d_model = w_in.shape[0]
    sizes = [qa_w, kv_a, kv_a, qb_w] + [kv_b] * 6 + [B_HEADS * 3, d_model, d_model]
    offs = np.concatenate([[0], np.cumsum(sizes)])
    qa, ka, va, qb, kc, vc, ksl, vsl, kwn, vwn, gb, ga, gbm = [
        w_in[:, int(offs[i]):int(offs[i + 1])] for i in range(len(sizes))]
    gb_pad = jnp.zeros((d_model, 2 * LANES - B_HEADS * 3), w_in.dtype)
    w_a = jnp.concatenate([qa, ka], axis=1).astype(bf16)
    w_b = jnp.concatenate([qb, ksl, kwn], axis=1).astype(bf16)
    w_c = jnp.concatenate([va, kc, vc, vsl, vwn, gb, gb_pad, ga, gbm], axis=1).astype(bf16)
    return w_a, w_b, w_c


def _layer(h, tabs, w, *, batch, seq):
    n, d_model = h.shape
    c64, s64, c128, s128 = tabs
    kv_a = A_KV_HEADS * A_HEAD_DIM
    kv_b = B_KV_GROUPS * B_HEAD_DIM
    qb_w = B_HEADS * B_HEAD_DIM
    rep_b = B_HEADS // B_KV_GROUPS

    h = _ffn(h, w["ffn1_norm"], w["ffn1_w_gate"], w["ffn1_w_up"], w["ffn1_w_down"], tm=512, tf=256)

    xn = _norm_cast(h, w["mix_norm"], tm=512)
    w_a, w_b, w_c = w["w_in_split"]
    proj_a = _proj(xn, w_a, tm=1024, tn=768, mode="rope64", tables=(c64, s64))
    proj_b_rot, proj_b_raw = _proj(xn, w_b, tm=1024, tn=512, mode="rope128", tables=(c128, s128),
                                   n_scaled_tiles=qb_w // 512, scale=B_HEAD_DIM ** -0.5)
    proj_c = _proj(xn, w_c, tm=1024, tn=512, mode="plain")

    o_a = _banded(proj_a, proj_a, proj_c, w["sinks"], batch=batch, seq=seq,
                  q_col=0, k_col=(A_HEADS * A_HEAD_DIM) // kv_a, v_col=0,
                  groups=A_KV_HEADS, rep=A_HEADS // A_KV_HEADS, dim=A_HEAD_DIM,
                  window=A_WINDOW, use_sink=True, q_scale=A_HEAD_DIM ** -0.5)

    ncp = seq // CMP_STRIDE
    nc = (seq - CMP_BLOCK) // CMP_STRIDE + 1
    ns = seq // SLC_BLOCK
    kvc = proj_c[:, kv_a:kv_a + 2 * kv_b].reshape(batch, ncp, CMP_STRIDE * 2 * kv_b)
    kcmp, vcmp = _compress(kvc, w["nsa_pe_k"], w["nsa_w_ck1"], w["nsa_w_ck2"],
                           w["nsa_pe_v"], w["nsa_w_cv1"], w["nsa_w_cv2"],
                           batch=batch, groups=B_KV_GROUPS, dim=B_HEAD_DIM, n_real=nc)
    sel_t = _selection_map_t(nc, ns, ncp, LANES)
    o_cmp, mbias = _cmp_attention(proj_b_raw, kcmp, vcmp, sel_t, batch=batch, seq=seq,
                                  groups=B_KV_GROUPS, rep=rep_b, dim=B_HEAD_DIM, tq=128,
                                  n_sel=min(N_SELECT, ns))
    o_slc = _slc_attention(proj_b_rot, mbias, proj_b_rot, proj_c, batch=batch, seq=seq,
                           groups=B_KV_GROUPS, rep=rep_b, dim=B_HEAD_DIM, tq=128,
                           kt=min(512, seq),
                           k_col=qb_w // B_HEAD_DIM, v_col=(kv_a + 2 * kv_b) // B_HEAD_DIM)
    o_win = _banded(proj_b_rot, proj_b_rot, proj_c, w["sinks"], batch=batch, seq=seq,
                    q_col=0, k_col=(qb_w + kv_b) // kv_b, v_col=(kv_a + 3 * kv_b) // kv_b,
                    groups=B_KV_GROUPS, rep=rep_b, dim=B_HEAD_DIM, window=B_WINDOW,
                    use_sink=False)

    gb_off = kv_a + 4 * kv_b
    tn = 512
    merged = _merge(o_a, o_cmp, o_slc, o_win, proj_c, w["w_o_a"], w["w_o_b"], tm=512, tn=tn,
                    gb_col=gb_off // LANES, ga_col=(gb_off + 2 * LANES) // tn,
                    gbm_col=(gb_off + 2 * LANES + d_model) // tn)
    h = _matmul_res(merged, w["w_o"], h, tm=1024, tn=512)

    h = _ffn(h, w["ffn2_norm"], w["ffn2_w_gate"], w["ffn2_w_up"], w["ffn2_w_down"], tm=512, tf=256)
    return h


def kernel(x, p, positions, ffn1_norm, ffn1_w_gate, ffn1_w_up, ffn1_w_down, mix_norm, w_in, sinks, nsa_pe_k, nsa_w_ck1, nsa_w_ck2, nsa_pe_v, nsa_w_cv1, nsa_w_cv2, w_o_a, w_o_b, w_o, ffn2_norm, ffn2_w_gate, ffn2_w_up, ffn2_w_down, ple_norm, w_ple_gate, w_ple_proj, final_norm):
    batch, seq, d_model = x.shape
    depth = ffn1_norm.shape[0]
    assert depth == 1, "the final norm is fused into the last layer's embedding step"
    assert seq // SLC_BLOCK <= LANES and seq % 512 == 0
    n = batch * seq
    h = x.reshape(n, d_model)
    tabs = _rope_tables(positions, tm=1024)
    i = 0
    cast = lambda a: a[i].astype(bf16)
    w = dict(
        ffn1_norm=ffn1_norm[i], ffn1_w_gate=cast(ffn1_w_gate), ffn1_w_up=cast(ffn1_w_up),
        ffn1_w_down=cast(ffn1_w_down), mix_norm=mix_norm[i], w_in_split=_split_w_in(w_in[i]),
        sinks=sinks[i], nsa_pe_k=nsa_pe_k[i], nsa_w_ck1=cast(nsa_w_ck1), nsa_w_ck2=cast(nsa_w_ck2),
        nsa_pe_v=nsa_pe_v[i], nsa_w_cv1=cast(nsa_w_cv1), nsa_w_cv2=cast(nsa_w_cv2),
        w_o_a=cast(w_o_a), w_o_b=cast(w_o_b), w_o=cast(w_o),
        ffn2_norm=ffn2_norm[i], ffn2_w_gate=cast(ffn2_w_gate), ffn2_w_up=cast(ffn2_w_up),
        ffn2_w_down=cast(ffn2_w_down))
    h = _layer(h, tabs, w, batch=batch, seq=seq)
    out = _ple(_norm_cast(h, ple_norm[i], tm=512), h, p[i].reshape(n, -1), cast(w_ple_gate),
               cast(w_ple_proj), final_norm, tm=512, tn=512)
    return out.reshape(batch, seq, d_model)
```

```python
import functools

import numpy as np
import jax
import jax.numpy as jnp
from jax import lax
from jax.experimental import pallas as pl
from jax.experimental.pallas import tpu as pltpu

f32 = jnp.float32
bf16 = jnp.bfloat16

A_HEADS, A_KV_HEADS, A_HEAD_DIM, A_WINDOW = 32, 4, 64, 128
B_HEADS, B_KV_GROUPS, B_HEAD_DIM = 16, 2, 128
CMP_BLOCK, CMP_STRIDE = 32, 16
SLC_BLOCK, N_SELECT, B_WINDOW = 64, 16, 512
BAND_BLOCK = 128
ROPE_THETA = 10000.0
EPS = 1e-6

LANES = 128
VMEM_LIMIT = 56 * 1024 * 1024
FFN_TILE = 256
MASKED = -1e30
M_INIT = -1e29

_NT = (((1,), (1,)), ((), ()))
_TN = (((0,), (0,)), ((), ()))


def _params(sem):
    return pltpu.CompilerParams(dimension_semantics=sem, vmem_limit_bytes=VMEM_LIMIT)


def _rmsnorm(x, g):
    var = jnp.mean(x * x, axis=-1, keepdims=True)
    return x * lax.rsqrt(var + EPS) * g


def _sigmoid(x):
    return 1.0 / (1.0 + jnp.exp(-x))


def _lane_tile(x, n):
    return x if n == 1 else jnp.concatenate([x] * n, axis=1)


def _cast_kernel(w_ref, o_ref):
    o_ref[...] = w_ref[...].astype(o_ref.dtype)


def _cast_bf16(w, *, bk, bn, tiled=False):
    k, n = w.shape
    if tiled:
        out = jax.ShapeDtypeStruct((n // bn, k, bn), bf16)
        o_spec = pl.BlockSpec((None, bk, bn), lambda i, j: (j, i, 0))
    else:
        out = jax.ShapeDtypeStruct((k, n), bf16)
        o_spec = pl.BlockSpec((bk, bn), lambda i, j: (i, j))
    return pl.pallas_call(
        _cast_kernel,
        out_shape=out,
        grid=(k // bk, n // bn),
        in_specs=[pl.BlockSpec((bk, bn), lambda i, j: (i, j))],
        out_specs=o_spec,
        compiler_params=_params(("parallel", "parallel")),
        name="cast_bf16",
    )(w)


def _ffn_kernel(x_ref, g_ref, wg_ref, wu_ref, wd_ref, o_ref, xn_ref):
    @pl.when(pl.program_id(1) == 0)
    def _():
        x = x_ref[...]
        xn_ref[...] = _rmsnorm(x, g_ref[...]).astype(bf16)
        o_ref[...] = x

    xn = xn_ref[...]
    gate = jnp.dot(xn, wg_ref[...], preferred_element_type=f32)
    up = jnp.dot(xn, wu_ref[...], preferred_element_type=f32)
    act = (0.5 * gate * _sigmoid(gate) * up).astype(bf16)
    o_ref[...] += jnp.dot(act, wd_ref[...], preferred_element_type=f32)


def _ffn(x, g, wg, wu, wd, *, tm):
    n, d = x.shape
    n_tiles, _, tf = wg.shape
    return pl.pallas_call(
        _ffn_kernel,
        out_shape=jax.ShapeDtypeStruct((n, d), f32),
        grid=(n // tm, n_tiles),
        in_specs=[
            pl.BlockSpec((tm, d), lambda i, j: (i, 0), pipeline_mode=pl.Buffered(1)),
            pl.BlockSpec((1, d), lambda i, j: (0, 0)),
            pl.BlockSpec((None, d, tf), lambda i, j: (j, 0, 0)),
            pl.BlockSpec((None, d, tf), lambda i, j: (j, 0, 0)),
            pl.BlockSpec((tf, d), lambda i, j: (j, 0)),
        ],
        out_specs=pl.BlockSpec((tm, d), lambda i, j: (i, 0)),
        scratch_shapes=[pltpu.VMEM((tm, d), bf16)],
        compiler_params=_params(("parallel", "arbitrary")),
        name="ffn",
    )(x, g.reshape(1, d), wg, wu, wd)


def _norm_cast_kernel(x_ref, g_ref, o_ref):
    o_ref[...] = _rmsnorm(x_ref[...], g_ref[...]).astype(bf16)


def _norm_cast(x, g, *, tm):
    n, d = x.shape
    return pl.pallas_call(
        _norm_cast_kernel,
        out_shape=jax.ShapeDtypeStruct((n, d), bf16),
        grid=(n // tm,),
        in_specs=[pl.BlockSpec((tm, d), lambda i: (i, 0)),
                  pl.BlockSpec((1, d), lambda i: (0, 0))],
        out_specs=pl.BlockSpec((tm, d), lambda i: (i, 0)),
        compiler_params=_params(("parallel",)),
        name="norm_cast",
    )(x, g.reshape(1, d))


def _rope_table_kernel(pos_ref, inv64_ref, inv128_ref, c64_ref, s64_ref, c128_ref, s128_ref):
    pos = pos_ref[...].astype(f32)
    lane = lax.broadcasted_iota(jnp.int32, c64_ref.shape, 1)
    a64 = pos * inv64_ref[...]
    a128 = pos * inv128_ref[...]
    c64_ref[...] = jnp.cos(a64)
    s64_ref[...] = jnp.where(lane % A_HEAD_DIM < A_HEAD_DIM // 2, -1.0, 1.0) * jnp.sin(a64)
    c128_ref[...] = jnp.cos(a128)
    s128_ref[...] = jnp.where(lane < B_HEAD_DIM // 2, -1.0, 1.0) * jnp.sin(a128)


def _rope_tables(positions, *, tm):
    n = positions.size

    def inv(d):
        v = jnp.power(jnp.float32(ROPE_THETA), -jnp.arange(0, d, 2, dtype=f32) / d)
        return jnp.tile(v, LANES // (d // 2)).reshape(1, LANES)

    tab = jax.ShapeDtypeStruct((n, LANES), f32)
    row = pl.BlockSpec((1, LANES), lambda i: (0, 0))
    blk = pl.BlockSpec((tm, LANES), lambda i: (i, 0))
    return pl.pallas_call(
        _rope_table_kernel,
        out_shape=(tab, tab, tab, tab),
        grid=(n // tm,),
        in_specs=[pl.BlockSpec((tm, 1), lambda i: (i, 0)), row, row],
        out_specs=(blk, blk, blk, blk),
        compiler_params=_params(("parallel",)),
        name="rope_tables",
    )(positions.reshape(n, 1), inv(A_HEAD_DIM), inv(B_HEAD_DIM))


def _proj_plain_kernel(x_ref, w_ref, o_ref):
    o_ref[...] = jnp.dot(x_ref[...], w_ref[...], preferred_element_type=f32).astype(o_ref.dtype)


def _proj_rope64_kernel(x_ref, w_ref, c_ref, s_ref, o_ref):
    acc = jnp.dot(x_ref[...], w_ref[...], preferred_element_type=f32)
    tn = acc.shape[1]
    half = A_HEAD_DIM // 2
    first = lax.broadcasted_iota(jnp.int32, c_ref.shape, 1) % A_HEAD_DIM < half
    cos, sin = c_ref[...], s_ref[...]
    for c in range(tn // LANES):
        seg = acc[:, c * LANES:(c + 1) * LANES]
        rot = jnp.where(first, pltpu.roll(seg, LANES - half, 1), pltpu.roll(seg, half, 1))
        o_ref[:, c * LANES:(c + 1) * LANES] = (seg * cos + rot * sin).astype(o_ref.dtype)


def _proj_rope128_kernel(x_ref, w_ref, c_ref, s_ref, o_rot_ref, o_raw_ref, *, n_scaled_tiles, scale):
    acc = jnp.dot(x_ref[...], w_ref[...], preferred_element_type=f32)
    acc = acc * jnp.where(pl.program_id(1) < n_scaled_tiles, scale, 1.0).astype(f32)
    tn = acc.shape[1]
    cos, sin = c_ref[...], s_ref[...]
    o_raw_ref[...] = acc.astype(o_raw_ref.dtype)
    for c in range(tn // LANES):
        seg = acc[:, c * LANES:(c + 1) * LANES]
        rot = pltpu.roll(seg, B_HEAD_DIM // 2, 1)
        o_rot_ref[:, c * LANES:(c + 1) * LANES] = (seg * cos + rot * sin).astype(o_rot_ref.dtype)


def _proj(xn, w, *, tm, tn, mode, tables=None, n_scaled_tiles=0, scale=1.0):
    n, d = xn.shape
    width = w.shape[1]
    x_spec = pl.BlockSpec((tm, d), lambda i, j: (i, 0))
    w_spec = pl.BlockSpec((d, tn), lambda i, j: (0, j))
    o_spec = pl.BlockSpec((tm, tn), lambda i, j: (i, j))
    t_spec = pl.BlockSpec((tm, LANES), lambda i, j: (i, 0))
    out = jax.ShapeDtypeStruct((n, width), bf16)
    common = dict(grid=(n // tm, width // tn),
                  compiler_params=_params(("parallel", "arbitrary")))
    if mode == "plain":
        return pl.pallas_call(_proj_plain_kernel, out_shape=out, in_specs=[x_spec, w_spec],
                              out_specs=o_spec, name="proj_plain", **common)(xn, w)
    if mode == "rope64":
        return pl.pallas_call(_proj_rope64_kernel, out_shape=out,
                              in_specs=[x_spec, w_spec, t_spec, t_spec],
                              out_specs=o_spec, name="proj_rope64", **common)(xn, w, *tables)
    kern = functools.partial(_proj_rope128_kernel, n_scaled_tiles=n_scaled_tiles, scale=scale)
    return pl.pallas_call(kern, out_shape=(out, out),
                          in_specs=[x_spec, w_spec, t_spec, t_spec],
                          out_specs=(o_spec, o_spec), name="proj_rope128", **common)(xn, w, *tables)


def _banded_kernel(sink_ref, q_ref, k_ref, v_ref, o_ref, *, groups, rep, dim, window, prev,
                   tq, use_sink, q_scale):
    q0 = pl.program_id(1) * tq
    slab = prev + tq
    pack = LANES // dim
    cols = rep // pack
    start = pl.multiple_of(jnp.maximum(q0 - prev, 0), tq)
    kpos = start + lax.broadcasted_iota(jnp.int32, (slab, tq), 0)
    tpos = q0 + lax.broadcasted_iota(jnp.int32, (slab, tq), 1)
    diff = tpos - kpos
    bias = _lane_tile(jnp.where((diff >= 0) & (diff < window), 0.0, -jnp.inf), cols)
    zeros = jnp.zeros((slab, LANES - dim), bf16)
    for g in range(groups):
        q = jnp.concatenate([q_ref[:, (g * cols + c) * LANES:(g * cols + c + 1) * LANES]
                             for c in range(cols)], axis=0)
        if q_scale != 1.0:
            q = q * q_scale
        kg = k_ref[pl.ds(start, slab), g * dim:(g + 1) * dim]
        vg = v_ref[pl.ds(start, slab), g * dim:(g + 1) * dim]
        out_t = None
        for sub in range(pack):
            if pack == 1:
                kk, vv = kg, vg
            else:
                parts = lambda a: [a, zeros] if sub == 0 else [zeros, a]
                kk, vv = jnp.concatenate(parts(kg), axis=1), jnp.concatenate(parts(vg), axis=1)
            s = lax.dot_general(kk, q, _NT, preferred_element_type=f32) + bias
            m = jnp.max(s, axis=0, keepdims=True)
            if use_sink:
                sk = jnp.concatenate([jnp.full((1, tq), sink_ref[g * rep + pack * c + sub], f32)
                                      for c in range(cols)], axis=1)
                m = jnp.maximum(m, sk)
            e = jnp.exp(s - m)
            den = jnp.sum(e, axis=0, keepdims=True)
            if use_sink:
                den = den + jnp.exp(sk - m)
            o_t = lax.dot_general(vv, e.astype(bf16), _TN, preferred_element_type=f32)
            o_t = o_t * (1.0 / den)
            out_t = o_t if out_t is None else out_t + o_t
        for c in range(cols):
            o_ref[:, (g * cols + c) * LANES:(g * cols + c + 1) * LANES] = (
                out_t[:, c * tq:(c + 1) * tq].T.astype(o_ref.dtype))


def _banded(q, k, v, sinks, *, batch, seq, q_col, k_col, v_col, groups, rep, dim, window,
            use_sink, q_scale=1.0):
    tq = BAND_BLOCK
    prev = -(-(window - 1) // tq) * tq
    nq = seq // tq
    hw, gw = groups * rep * dim, groups * dim
    assert LANES % dim == 0 and rep % (LANES // dim) == 0
    kern = functools.partial(_banded_kernel, groups=groups, rep=rep, dim=dim, window=window,
                             prev=prev, tq=tq, use_sink=use_sink, q_scale=q_scale)
    return pl.pallas_call(
        kern,
        out_shape=jax.ShapeDtypeStruct((batch * seq, hw), bf16),
        grid=(batch, nq),
        in_specs=[
            pl.BlockSpec(memory_space=pltpu.SMEM),
            pl.BlockSpec((tq, hw), lambda b, i: (b * nq + i, q_col)),
            pl.BlockSpec((seq, gw), lambda b, i: (b, k_col)),
            pl.BlockSpec((seq, gw), lambda b, i: (b, v_col)),
        ],
        out_specs=pl.BlockSpec((tq, hw), lambda b, i: (b * nq + i, 0)),
        compiler_params=_params(("parallel", "arbitrary")),
        name="banded_sink" if use_sink else "banded",
    )(sinks, q, k, v)


def _gelu_tanh(x):
    return x * (0.5 * (1.0 + jnp.tanh(np.sqrt(2.0 / np.pi).astype(np.float32)
                                      * (x + 0.044715 * (x * x * x)))))


def _compress_kernel(r_ref, pek_ref, wk1_ref, wk2_ref, pev_ref, wv1_ref, wv2_ref,
                     kc_ref, vc_ref, *, groups, dim, n_real):
    ncp = r_ref.shape[0]
    tok_w = 2 * groups * dim
    halves = CMP_BLOCK // CMP_STRIDE
    row = lax.broadcasted_iota(jnp.int32, (ncp, dim), 0)
    for which, (pe_ref, w1_ref, w2_ref, out_ref) in enumerate(
            ((pek_ref, wk1_ref, wk2_ref, kc_ref), (pev_ref, wv1_ref, wv2_ref, vc_ref))):
        for g in range(groups):
            h1 = None
            for half in range(halves):
                pieces = []
                for tl in range(CMP_STRIDE):
                    l = half * CMP_STRIDE + tl
                    col = tl * tok_w + which * groups * dim + g * dim
                    pieces.append((r_ref[:, col:col + dim].astype(f32) + pe_ref[l:l + 1, :]).astype(bf16))
                x = jnp.concatenate(pieces, axis=1)
                w = w1_ref[half * CMP_STRIDE * dim:(half + 1) * CMP_STRIDE * dim, :]
                part = jnp.dot(x, w, preferred_element_type=f32)
                if half:
                    part = pltpu.roll(part, ncp - half, 0)
                h1 = part if h1 is None else h1 + part
            y = _gelu_tanh(h1).astype(bf16)
            out = jnp.dot(y, w2_ref[...], preferred_element_type=f32)
            out_ref[g] = jnp.where(row < n_real, out, 0.0).astype(out_ref.dtype)


def _compress(r, pe_k, wk1, wk2, pe_v, wv1, wv2, *, batch, groups, dim, n_real):
    _, ncp, width = r.shape
    hidden = wk1.shape[1]
    full = lambda shape: pl.BlockSpec(shape, lambda b: (0,) * len(shape))
    out = jax.ShapeDtypeStruct((batch, groups, ncp, dim), bf16)
    o_spec = pl.BlockSpec((None, groups, ncp, dim), lambda b: (b, 0, 0, 0))
    kern = functools.partial(_compress_kernel, groups=groups, dim=dim, n_real=n_real)
    return pl.pallas_call(
        kern,
        out_shape=(out, out),
        grid=(batch,),
        in_specs=[pl.BlockSpec((None, ncp, width), lambda b: (b, 0, 0)),
                  full((CMP_BLOCK, dim)), full((CMP_BLOCK * dim, hidden)), full((hidden, dim)),
                  full((CMP_BLOCK, dim)), full((CMP_BLOCK * dim, hidden)), full((hidden, dim))],
        out_specs=(o_spec, o_spec),
        compiler_params=_params(("parallel",)),
        name="compress",
    )(r, pe_k, wk1, wk2, pe_v, wv1, wv2)


def _cmp_kernel(q_ref, kc_ref, vc_ref, selt_ref, o_ref, mb_ref, *, rep, dim, tq, n_sel):
    q0 = pl.program_id(2) * tq
    ncp = kc_ref.shape[0]
    nsp = selt_ref.shape[0]
    q = jnp.concatenate([q_ref[:, r * dim:(r + 1) * dim] for r in range(rep)], axis=0)
    cidx = lax.broadcasted_iota(jnp.int32, (ncp, tq), 0)
    tpos = q0 + lax.broadcasted_iota(jnp.int32, (ncp, tq), 1)
    bias = _lane_tile(jnp.where(cidx * CMP_STRIDE + (CMP_BLOCK - 1) <= tpos, 0.0, -jnp.inf), rep)
    s = lax.dot_general(kc_ref[...], q, _NT, preferred_element_type=f32) + bias
    m = jnp.max(s, axis=0, keepdims=True)
    m = jnp.where(m > -jnp.inf, m, 0.0)
    e = jnp.exp(s - m)
    den = jnp.sum(e, axis=0, keepdims=True)
    p = e * (1.0 / jnp.where(den > 0, den, 1.0))
    o_t = lax.dot_general(vc_ref[...], p.astype(bf16), _TN, preferred_element_type=f32)
    psum = None
    for r in range(rep):
        o_ref[:, r * dim:(r + 1) * dim] = o_t[:, r * tq:(r + 1) * tq].T.astype(o_ref.dtype)
        pr = p[:, r * tq:(r + 1) * tq]
        psum = pr if psum is None else psum + pr

    imp = jnp.dot(selt_ref[...], psum, preferred_element_type=f32, precision=lax.Precision.HIGHEST)
    j = lax.broadcasted_iota(jnp.int32, (nsp, tq), 0)
    tt = q0 + lax.broadcasted_iota(jnp.int32, (nsp, tq), 1)
    cur = tt // SLC_BLOCK
    forced = (j == 0) | (j == cur) | (j == cur - 1)
    valid = j * SLC_BLOCK <= tt
    score = jnp.where(forced, jnp.inf, jnp.where(valid, imp, -jnp.inf))
    chosen = jnp.zeros((nsp, tq), f32)
    jf = j.astype(f32)
    for _ in range(n_sel):
        best = jnp.max(score, axis=0, keepdims=True)
        first = jnp.min(jnp.where(score == best, jf, float(nsp)), axis=0, keepdims=True)
        hit = jf == first
        chosen = jnp.where(hit, 1.0, chosen)
        score = jnp.where(hit, -jnp.inf, score)
    mb_ref[...] = jnp.where(chosen > 0, 0.0, MASKED).T.astype(mb_ref.dtype)


def _cmp_attention(q_raw, kcmp, vcmp, sel_t, *, batch, seq, groups, rep, dim, tq, n_sel):
    nq = seq // tq
    ncp = kcmp.shape[2]
    nsp = sel_t.shape[0]
    kern = functools.partial(_cmp_kernel, rep=rep, dim=dim, tq=tq, n_sel=n_sel)
    kv_spec = pl.BlockSpec((None, None, ncp, dim), lambda b, g, i: (b, g, 0, 0))
    return pl.pallas_call(
        kern,
        out_shape=(jax.ShapeDtypeStruct((batch * seq, groups * rep * dim), bf16),
                   jax.ShapeDtypeStruct((batch, groups, seq, nsp), bf16)),
        grid=(batch, groups, nq),
        in_specs=[pl.BlockSpec((tq, rep * dim), lambda b, g, i: (b * nq + i, g)),
                  kv_spec, kv_spec,
                  pl.BlockSpec((nsp, ncp), lambda b, g, i: (0, 0))],
        out_specs=(pl.BlockSpec((tq, rep * dim), lambda b, g, i: (b * nq + i, g)),
                   pl.BlockSpec((None, None, tq, nsp), lambda b, g, i: (b, g, i, 0))),
        compiler_params=_params(("parallel", "parallel", "arbitrary")),
        name="cmp_attention",
    )(q_raw, kcmp, vcmp, sel_t)


def _slc_kernel(q_ref, mb_ref, k_ref, v_ref, o_ref, kp_ref, qp_ref, *, groups, rep, dim, tq, kt):
    i = pl.program_id(1)
    seq = k_ref.shape[0]
    nsp = mb_ref.shape[-1]
    rows = rep * tq

    @pl.when(i == 0)
    def _():
        blk = lax.broadcasted_iota(jnp.int32, (seq, nsp), 0) // SLC_BLOCK
        lane = lax.broadcasted_iota(jnp.int32, (seq, nsp), 1)
        onehot = jnp.where(blk == lane, 1.0, 0.0).astype(bf16)
        for g in range(groups):
            kp_ref[g, :, :dim] = k_ref[:, g * dim:(g + 1) * dim]
            kp_ref[g, :, dim:] = onehot

    for g in range(groups):
        mb = mb_ref[g]
        for r in range(rep):
            h = g * rep + r
            qp_ref[g, r * tq:(r + 1) * tq, :dim] = q_ref[:, h * dim:(h + 1) * dim]
            qp_ref[g, r * tq:(r + 1) * tq, dim:] = mb

    def update(g, carry, k0, bias):
        m, l, acc = carry
        s = lax.dot_general(kp_ref[g, pl.ds(k0, kt), :], qp_ref[g], _NT, preferred_element_type=f32)
        if bias is not None:
            s = s + bias
        m_new = jnp.maximum(m, jnp.max(s, axis=0, keepdims=True))
        alpha = jnp.exp(m - m_new)
        p = jnp.exp(s - m_new)
        l = alpha * l + jnp.sum(p, axis=0, keepdims=True)
        acc = alpha * acc + lax.dot_general(v_ref[pl.ds(k0, kt), g * dim:(g + 1) * dim], p.astype(bf16),
                                            _TN, preferred_element_type=f32)
        return m_new, l, acc

    def body(t, carries):
        k0 = pl.multiple_of(t * kt, kt)
        return tuple(update(g, carries[g], k0, None) for g in range(groups))

    n_full = (i * tq) // kt
    init = (jnp.full((1, rows), M_INIT, f32), jnp.zeros((1, rows), f32), jnp.zeros((dim, rows), f32))
    carries = lax.fori_loop(0, n_full, body, (init,) * groups)
    k0 = pl.multiple_of(n_full * kt, kt)
    kpos = k0 + lax.broadcasted_iota(jnp.int32, (kt, tq), 0)
    tpos = i * tq + lax.broadcasted_iota(jnp.int32, (kt, tq), 1)
    causal = _lane_tile(jnp.where(kpos <= tpos, 0.0, MASKED), rep)
    for g in range(groups):
        _, l, acc = update(g, carries[g], k0, causal)
        o_t = acc * (1.0 / l)
        for r in range(rep):
            h = g * rep + r
            o_ref[:, h * dim:(h + 1) * dim] = o_t[:, r * tq:(r + 1) * tq].T.astype(o_ref.dtype)


def _slc_attention(q_rot, mbias, k_rot, v, *, batch, seq, groups, rep, dim, tq, kt,
                   k_col, v_col):
    nq = seq // tq
    nsp = mbias.shape[-1]
    hw, gw = groups * rep * dim, groups * dim
    kern = functools.partial(_slc_kernel, groups=groups, rep=rep, dim=dim, tq=tq, kt=kt)
    return pl.pallas_call(
        kern,
        out_shape=jax.ShapeDtypeStruct((batch * seq, hw), bf16),
        grid=(batch, nq),
        in_specs=[pl.BlockSpec((tq, hw), lambda b, i: (b * nq + i, 0)),
                  pl.BlockSpec((None, groups, tq, nsp), lambda b, i: (b, 0, i, 0)),
                  pl.BlockSpec((seq, gw), lambda b, i: (b, k_col)),
                  pl.BlockSpec((seq, gw), lambda b, i: (b, v_col))],
        out_specs=pl.BlockSpec((tq, hw), lambda b, i: (b * nq + i, 0)),
        scratch_shapes=[pltpu.VMEM((groups, seq, dim + nsp), bf16),
                        pltpu.VMEM((groups, rep * tq, dim + nsp), bf16)],
        compiler_params=_params(("parallel", "arbitrary")),
        name="slc_attention",
    )(q_rot, mbias, k_rot, v)


def _merge_kernel(oa_ref, oc_ref, os_ref, ow_ref, gb_ref, ga_ref, gbm_ref, woa_ref, wob_ref,
                  o_ref, ob_ref, *, heads, dim):
    @pl.when(pl.program_id(1) == 0)
    def _():
        g = _sigmoid(gb_ref[...].astype(f32))
        for h in range(heads):
            cols = slice(h * dim, (h + 1) * dim)
            ob = (g[:, 3 * h:3 * h + 1] * oc_ref[:, cols].astype(f32)
                  + g[:, 3 * h + 1:3 * h + 2] * os_ref[:, cols].astype(f32)
                  + g[:, 3 * h + 2:3 * h + 3] * ow_ref[:, cols].astype(f32))
            ob_ref[:, cols] = ob.astype(bf16)

    ya = jnp.dot(oa_ref[...], woa_ref[...], preferred_element_type=f32)
    yb = jnp.dot(ob_ref[...], wob_ref[...], preferred_element_type=f32)
    merged = _sigmoid(ga_ref[...].astype(f32)) * ya + _sigmoid(gbm_ref[...].astype(f32)) * yb
    o_ref[...] = merged.astype(o_ref.dtype)


def _merge(o_a, o_cmp, o_slc, o_win, proj_c, w_oa, w_ob, *, tm, tn, gb_col, ga_col, gbm_col):
    n, da = o_a.shape
    db = o_cmp.shape[1]
    d = w_oa.shape[1]
    row = lambda w: pl.BlockSpec((tm, w), lambda i, j: (i, 0))
    kern = functools.partial(_merge_kernel, heads=B_HEADS, dim=B_HEAD_DIM)
    return pl.pallas_call(
        kern,
        out_shape=jax.ShapeDtypeStruct((n, d), bf16),
        grid=(n // tm, d // tn),
        in_specs=[row(da), row(db), row(db), row(db),
                  pl.BlockSpec((tm, LANES), lambda i, j: (i, gb_col)),
                  pl.BlockSpec((tm, tn), lambda i, j: (i, ga_col + j)),
                  pl.BlockSpec((tm, tn), lambda i, j: (i, gbm_col + j)),
                  pl.BlockSpec((da, tn), lambda i, j: (0, j)),
                  pl.BlockSpec((db, tn), lambda i, j: (0, j))],
        out_specs=pl.BlockSpec((tm, tn), lambda i, j: (i, j)),
        scratch_shapes=[pltpu.VMEM((tm, db), bf16)],
        compiler_params=_params(("parallel", "arbitrary")),
        name="merge",
    )(o_a, o_cmp, o_slc, o_win, proj_c, proj_c, proj_c, w_oa, w_ob)


def _matmul_res_kernel(x_ref, w_ref, r_ref, o_ref):
    o_ref[...] = r_ref[...] + jnp.dot(x_ref[...], w_ref[...], preferred_element_type=f32)


def _matmul_res(x, w, res, *, tm, tn):
    n, k = x.shape
    d = w.shape[1]
    return pl.pallas_call(
        _matmul_res_kernel,
        out_shape=jax.ShapeDtypeStruct((n, d), f32),
        grid=(n // tm, d // tn),
        in_specs=[pl.BlockSpec((tm, k), lambda i, j: (i, 0)),
                  pl.BlockSpec((k, tn), lambda i, j: (0, j)),
                  pl.BlockSpec((tm, tn), lambda i, j: (i, j))],
        out_specs=pl.BlockSpec((tm, tn), lambda i, j: (i, j)),
        compiler_params=_params(("parallel", "arbitrary")),
        name="matmul_res",
    )(x, w, res)


def _ple_kernel(xn_ref, h_ref, p_ref, wg_ref, wp_ref, fn_ref, o_ref, h4_ref):
    j = pl.program_id(1)
    nj = pl.num_programs(1)
    tn = h_ref.shape[1]
    gate = _sigmoid(jnp.dot(xn_ref[...], wg_ref[...], preferred_element_type=f32))
    emb = jnp.dot(p_ref[...].astype(bf16), wp_ref[...], preferred_element_type=f32)
    h4_ref[j] = h_ref[...] + emb * gate

    @pl.when(j == nj - 1)
    def _():
        n_tiles = h4_ref.shape[0]
        ss = None
        for c in range(n_tiles):
            t = h4_ref[c]
            part = jnp.sum(t * t, axis=-1, keepdims=True)
            ss = part if ss is None else ss + part
        inv = lax.rsqrt(ss / (n_tiles * tn) + EPS)
        for c in range(n_tiles):
            o_ref[:, c * tn:(c + 1) * tn] = h4_ref[c] * inv * fn_ref[:, c * tn:(c + 1) * tn]


def _ple(xn, h, p, w_gate, w_proj, final_norm, *, tm, tn):
    n, d = h.shape
    pd = p.shape[1]
    return pl.pallas_call(
        _ple_kernel,
        out_shape=jax.ShapeDtypeStruct((n, d), f32),
        grid=(n // tm, d // tn),
        in_specs=[pl.BlockSpec((tm, d), lambda i, j: (i, 0)),
                  pl.BlockSpec((tm, tn), lambda i, j: (i, j)),
                  pl.BlockSpec((tm, pd), lambda i, j: (i, 0)),
                  pl.BlockSpec((d, tn), lambda i, j: (0, j)),
                  pl.BlockSpec((pd, tn), lambda i, j: (0, j)),
                  pl.BlockSpec((1, d), lambda i, j: (0, 0))],
        out_specs=pl.BlockSpec((tm, d), lambda i, j: (i, 0)),
        scratch_shapes=[pltpu.VMEM((d // tn, tm, tn), f32)],
        compiler_params=_params(("parallel", "arbitrary")),
        name="ple_final",
    )(xn, h, p, w_gate, w_proj, final_norm.reshape(1, d))


def _selection_map_t(nc, ns, ncp, nsp):
    a, b = SLC_BLOCK // CMP_STRIDE, CMP_BLOCK // CMP_STRIDE
    j = np.arange(ns)[:, None, None]
    c = a * j - np.arange(a)[None, :, None] - np.arange(b)[None, None, :]
    jj = np.broadcast_to(j, c.shape)
    ok = (c >= 0) & (c < nc)
    mat = np.zeros((nsp, ncp), np.float32)
    np.add.at(mat, (jj[ok], c[ok]), 1.0)
    return jnp.asarray(mat)


def _split_w_in(w_in):
    qa_w, kv_a = A_HEADS * A_HEAD_DIM, A_KV_HEADS * A_HEAD_DIM
    qb_w, kv_b = B_HEADS * B_HEAD_DIM, B_KV_GROUPS * B_HEAD_DIM
    d_model = w_in.shape[0]
    sizes = [qa_w, kv_a, kv_a, qb_w] + [kv_b] * 6 + [B_HEADS * 3, d_model, d_model]
    offs = np.concatenate([[0], np.cumsum(sizes)])
    qa, ka, va, qb, kc, vc, ksl, vsl, kwn, vwn, gb, ga, gbm = [
        w_in[:, int(offs[i]):int(offs[i + 1])] for i in range(len(sizes))]
    gb_pad = jnp.zeros((d_model, 2 * LANES - B_HEADS * 3), w_in.dtype)
    w_a = jnp.concatenate([qa, ka], axis=1).astype(bf16)
    w_b = jnp.concatenate([qb, ksl, kwn], axis=1).astype(bf16)
    w_c = jnp.concatenate([va, kc, vc, vsl, vwn, gb, gb_pad, ga, gbm], axis=1).astype(bf16)
    return w_a, w_b, w_c


def _layer(h, tabs, w, *, batch, seq):
    n, d_model = h.shape
    c64, s64, c128, s128 = tabs
    kv_a = A_KV_HEADS * A_HEAD_DIM
    kv_b = B_KV_GROUPS * B_HEAD_DIM
    qb_w = B_HEADS * B_HEAD_DIM
    rep_b = B_HEADS // B_KV_GROUPS

    h = _ffn(h, w["ffn1_norm"], w["ffn1_w_gate"], w["ffn1_w_up"], w["ffn1_w_down"], tm=512)

    xn = _norm_cast(h, w["mix_norm"], tm=512)
    w_a, w_b, w_c = w["w_in_split"]
    proj_a = _proj(xn, w_a, tm=1024, tn=768, mode="rope64", tables=(c64, s64))
    proj_b_rot, proj_b_raw = _proj(xn, w_b, tm=1024, tn=512, mode="rope128", tables=(c128, s128),
                                   n_scaled_tiles=qb_w // 512, scale=B_HEAD_DIM ** -0.5)
    proj_c = _proj(xn, w_c, tm=1024, tn=512, mode="plain")

    o_a = _banded(proj_a, proj_a, proj_c, w["sinks"], batch=batch, seq=seq,
                  q_col=0, k_col=(A_HEADS * A_HEAD_DIM) // kv_a, v_col=0,
                  groups=A_KV_HEADS, rep=A_HEADS // A_KV_HEADS, dim=A_HEAD_DIM,
                  window=A_WINDOW, use_sink=True, q_scale=A_HEAD_DIM ** -0.5)

    ncp = seq // CMP_STRIDE
    nc = (seq - CMP_BLOCK) // CMP_STRIDE + 1
    ns = seq // SLC_BLOCK
    kvc = proj_c[:, kv_a:kv_a + 2 * kv_b].reshape(batch, ncp, CMP_STRIDE * 2 * kv_b)
    kcmp, vcmp = _compress(kvc, w["nsa_pe_k"], w["nsa_w_ck1"], w["nsa_w_ck2"],
                           w["nsa_pe_v"], w["nsa_w_cv1"], w["nsa_w_cv2"],
                           batch=batch, groups=B_KV_GROUPS, dim=B_HEAD_DIM, n_real=nc)
    sel_t = _selection_map_t(nc, ns, ncp, LANES)
    o_cmp, mbias = _cmp_attention(proj_b_raw, kcmp, vcmp, sel_t, batch=batch, seq=seq,
                                  groups=B_KV_GROUPS, rep=rep_b, dim=B_HEAD_DIM, tq=128,
                                  n_sel=min(N_SELECT, ns))
    o_slc = _slc_attention(proj_b_rot, mbias, proj_b_rot, proj_c, batch=batch, seq=seq,
                           groups=B_KV_GROUPS, rep=rep_b, dim=B_HEAD_DIM, tq=256,
                           kt=min(512, seq),
                           k_col=qb_w // kv_b, v_col=(kv_a + 2 * kv_b) // kv_b)
    o_win = _banded(proj_b_rot, proj_b_rot, proj_c, w["sinks"], batch=batch, seq=seq,
                    q_col=0, k_col=(qb_w + kv_b) // kv_b, v_col=(kv_a + 3 * kv_b) // kv_b,
                    groups=B_KV_GROUPS, rep=rep_b, dim=B_HEAD_DIM, window=B_WINDOW,
                    use_sink=False)

    gb_off = kv_a + 4 * kv_b
    tn = 512
    merged = _merge(o_a, o_cmp, o_slc, o_win, proj_c, w["w_o_a"], w["w_o_b"], tm=512, tn=tn,
                    gb_col=gb_off // LANES, ga_col=(gb_off + 2 * LANES) // tn,
                    gbm_col=(gb_off + 2 * LANES + d_model) // tn)
    h = _matmul_res(merged, w["w_o"], h, tm=1024, tn=512)

    h = _ffn(h, w["ffn2_norm"], w["ffn2_w_gate"], w["ffn2_w_up"], w["ffn2_w_down"], tm=512)
    return h


def kernel(x, p, positions, ffn1_norm, ffn1_w_gate, ffn1_w_up, ffn1_w_down, mix_norm, w_in, sinks, nsa_pe_k, nsa_w_ck1, nsa_w_ck2, nsa_pe_v, nsa_w_cv1, nsa_w_cv2, w_o_a, w_o_b, w_o, ffn2_norm, ffn2_w_gate, ffn2_w_up, ffn2_w_down, ple_norm, w_ple_gate, w_ple_proj, final_norm):
    batch, seq, d_model = x.shape
    depth = ffn1_norm.shape[0]
    assert depth == 1, "the final norm is fused into the last layer's embedding step"
    assert seq // SLC_BLOCK <= LANES and seq % 512 == 0
    n = batch * seq
    h = x.reshape(n, d_model)
    tabs = _rope_tables(positions, tm=1024)
    i = 0
    small = lambda a: a[i].astype(bf16)
    rows = lambda a: _cast_bf16(a[i], bk=min(256, a.shape[1]), bn=a.shape[2])
    tiles = lambda a: _cast_bf16(a[i], bk=a.shape[1], bn=FFN_TILE, tiled=True)
    w = dict(
        ffn1_norm=ffn1_norm[i], ffn1_w_gate=tiles(ffn1_w_gate), ffn1_w_up=tiles(ffn1_w_up),
        ffn1_w_down=rows(ffn1_w_down), mix_norm=mix_norm[i], w_in_split=_split_w_in(w_in[i]),
        sinks=sinks[i], nsa_pe_k=nsa_pe_k[i], nsa_w_ck1=small(nsa_w_ck1), nsa_w_ck2=small(nsa_w_ck2),
        nsa_pe_v=nsa_pe_v[i], nsa_w_cv1=small(nsa_w_cv1), nsa_w_cv2=small(nsa_w_cv2),
        w_o_a=rows(w_o_a), w_o_b=rows(w_o_b), w_o=rows(w_o),
        ffn2_norm=ffn2_norm[i], ffn2_w_gate=tiles(ffn2_w_gate), ffn2_w_up=tiles(ffn2_w_up),
        ffn2_w_down=rows(ffn2_w_down))
    h = _layer(h, tabs, w, batch=batch, seq=seq)
    out = _ple(_norm_cast(h, ple_norm[i], tm=512), h, p[i].reshape(n, -1), rows(w_ple_gate),
               rows(w_ple_proj), final_norm, tm=512, tn=512)
    return out.reshape(batch, seq, d_model)
```

```python
import functools

import numpy as np
import jax
import jax.numpy as jnp
from jax import lax
from jax.experimental import pallas as pl
from jax.experimental.pallas import tpu as pltpu

f32 = jnp.float32
bf16 = jnp.bfloat16

A_HEADS, A_KV_HEADS, A_HEAD_DIM, A_WINDOW = 32, 4, 64, 128
B_HEADS, B_KV_GROUPS, B_HEAD_DIM = 16, 2, 128
CMP_BLOCK, CMP_STRIDE = 32, 16
SLC_BLOCK, N_SELECT, B_WINDOW = 64, 16, 512
BAND_BLOCK = 128
ROPE_THETA = 10000.0
EPS = 1e-6

LANES = 128
VMEM_LIMIT = 56 * 1024 * 1024
VMEM_LIMIT_FFN = 60 * 1024 * 1024
FFN_TILE = 512
FFN_DOWN_CHUNK = 1024
FFN_NORM_ROWS = 128
MASKED = -1e30
M_INIT = -1e29

_NT = (((1,), (1,)), ((), ()))
_TN = (((0,), (0,)), ((), ()))


def _params(sem, limit=VMEM_LIMIT):
    return pltpu.CompilerParams(dimension_semantics=sem, vmem_limit_bytes=limit)


def _rmsnorm(x, g):
    var = jnp.mean(x * x, axis=-1, keepdims=True)
    return x * lax.rsqrt(var + EPS) * g


def _sigmoid(x):
    return 1.0 / (1.0 + jnp.exp(-x))


def _lane_tile(x, n):
    return x if n == 1 else jnp.concatenate([x] * n, axis=1)


def _cast_kernel(w_ref, o_ref, *, k, n):
    bk, bn = w_ref.shape
    w = w_ref[...]
    if k % bk or n % bn:
        row = pl.program_id(0) * bk + lax.broadcasted_iota(jnp.int32, (bk, bn), 0)
        col = pl.program_id(1) * bn + lax.broadcasted_iota(jnp.int32, (bk, bn), 1)
        w = jnp.where((row < k) & (col < n), w, 0.0)
    o_ref[...] = w.astype(o_ref.dtype)


def _cast_bf16(w, *, bk, bn, tiled=False):
    k, n = w.shape
    gk, gn = pl.cdiv(k, bk), pl.cdiv(n, bn)
    if tiled:
        out = jax.ShapeDtypeStruct((gn, gk * bk, bn), bf16)
        o_spec = pl.BlockSpec((None, bk, bn), lambda i, j: (j, i, 0))
    else:
        out = jax.ShapeDtypeStruct((gk * bk, gn * bn), bf16)
        o_spec = pl.BlockSpec((bk, bn), lambda i, j: (i, j))
    return pl.pallas_call(
        functools.partial(_cast_kernel, k=k, n=n),
        out_shape=out,
        grid=(gk, gn),
        in_specs=[pl.BlockSpec((bk, bn), lambda i, j: (i, j))],
        out_specs=o_spec,
        compiler_params=_params(("parallel", "parallel")),
        name="cast_bf16",
    )(w)


def _ffn_kernel(x_ref, g_ref, wg_ref, wu_ref, wd_ref, o_ref, xn_ref):
    @pl.when(pl.program_id(1) == 0)
    def _():
        for c in range(x_ref.shape[0] // FFN_NORM_ROWS):
            rows = slice(c * FFN_NORM_ROWS, (c + 1) * FFN_NORM_ROWS)
            x = x_ref[rows, :]
            xn_ref[rows, :] = _rmsnorm(x, g_ref[...]).astype(bf16)
            o_ref[rows, :] = x

    xn = xn_ref[...]
    gate = jnp.dot(xn, wg_ref[...], preferred_element_type=f32)
    up = jnp.dot(xn, wu_ref[...], preferred_element_type=f32)
    act = (0.5 * gate * _sigmoid(gate) * up).astype(bf16)
    d = o_ref.shape[1]
    for c in range(d // FFN_DOWN_CHUNK):
        cols = slice(c * FFN_DOWN_CHUNK, (c + 1) * FFN_DOWN_CHUNK)
        o_ref[:, cols] += jnp.dot(act, wd_ref[:, cols], preferred_element_type=f32)


def _ffn(x, g, wg, wu, wd, *, tm):
    n, d = x.shape
    n_tiles, _, tf = wg.shape
    once = dict(pipeline_mode=pl.Buffered(1))
    return pl.pallas_call(
        _ffn_kernel,
        out_shape=jax.ShapeDtypeStruct((n, d), f32),
        grid=(n // tm, n_tiles),
        in_specs=[
            pl.BlockSpec((tm, d), lambda i, j: (i, 0), **once),
            pl.BlockSpec((1, d), lambda i, j: (0, 0)),
            pl.BlockSpec((None, d, tf), lambda i, j: (j, 0, 0)),
            pl.BlockSpec((None, d, tf), lambda i, j: (j, 0, 0)),
            pl.BlockSpec((tf, d), lambda i, j: (j, 0)),
        ],
        out_specs=pl.BlockSpec((tm, d), lambda i, j: (i, 0)),
        scratch_shapes=[pltpu.VMEM((tm, d), bf16)],
        compiler_params=_params(("parallel", "arbitrary"), VMEM_LIMIT_FFN),
        name="ffn",
    )(x, g.reshape(1, d), wg, wu, wd)


def _norm_cast_kernel(x_ref, g_ref, o_ref):
    o_ref[...] = _rmsnorm(x_ref[...], g_ref[...]).astype(bf16)


def _norm_cast(x, g, *, tm):
    n, d = x.shape
    return pl.pallas_call(
        _norm_cast_kernel,
        out_shape=jax.ShapeDtypeStruct((n, d), bf16),
        grid=(n // tm,),
        in_specs=[pl.BlockSpec((tm, d), lambda i: (i, 0)),
                  pl.BlockSpec((1, d), lambda i: (0, 0))],
        out_specs=pl.BlockSpec((tm, d), lambda i: (i, 0)),
        compiler_params=_params(("parallel",)),
        name="norm_cast",
    )(x, g.reshape(1, d))


def _rope_table_kernel(pos_ref, inv64_ref, inv128_ref, c64_ref, s64_ref, c128_ref, s128_ref):
    pos = pos_ref[...].astype(f32)
    lane = lax.broadcasted_iota(jnp.int32, c64_ref.shape, 1)
    a64 = pos * inv64_ref[...]
    a128 = pos * inv128_ref[...]
    c64_ref[...] = jnp.cos(a64)
    s64_ref[...] = jnp.where(lane % A_HEAD_DIM < A_HEAD_DIM // 2, -1.0, 1.0) * jnp.sin(a64)
    c128_ref[...] = jnp.cos(a128)
    s128_ref[...] = jnp.where(lane < B_HEAD_DIM // 2, -1.0, 1.0) * jnp.sin(a128)


def _rope_tables(positions, *, tm):
    n = positions.size

    def inv(d):
        v = jnp.power(jnp.float32(ROPE_THETA), -jnp.arange(0, d, 2, dtype=f32) / d)
        return jnp.tile(v, LANES // (d // 2)).reshape(1, LANES)

    tab = jax.ShapeDtypeStruct((n, LANES), f32)
    row = pl.BlockSpec((1, LANES), lambda i: (0, 0))
    blk = pl.BlockSpec((tm, LANES), lambda i: (i, 0))
    return pl.pallas_call(
        _rope_table_kernel,
        out_shape=(tab, tab, tab, tab),
        grid=(n // tm,),
        in_specs=[pl.BlockSpec((tm, 1), lambda i: (i, 0)), row, row],
        out_specs=(blk, blk, blk, blk),
        compiler_params=_params(("parallel",)),
        name="rope_tables",
    )(positions.reshape(n, 1), inv(A_HEAD_DIM), inv(B_HEAD_DIM))


def _proj_plain_kernel(x_ref, w_ref, o_ref):
    o_ref[...] = jnp.dot(x_ref[...], w_ref[...], preferred_element_type=f32).astype(o_ref.dtype)


def _proj_rope64_kernel(x_ref, w_ref, c_ref, s_ref, o_ref):
    acc = jnp.dot(x_ref[...], w_ref[...], preferred_element_type=f32)
    tn = acc.shape[1]
    half = A_HEAD_DIM // 2
    first = lax.broadcasted_iota(jnp.int32, c_ref.shape, 1) % A_HEAD_DIM < half
    cos, sin = c_ref[...], s_ref[...]
    for c in range(tn // LANES):
        seg = acc[:, c * LANES:(c + 1) * LANES]
        rot = jnp.where(first, pltpu.roll(seg, LANES - half, 1), pltpu.roll(seg, half, 1))
        o_ref[:, c * LANES:(c + 1) * LANES] = (seg * cos + rot * sin).astype(o_ref.dtype)


def _proj_rope128_kernel(x_ref, w_ref, c_ref, s_ref, o_rot_ref, o_raw_ref, *, n_scaled_tiles, scale):
    acc = jnp.dot(x_ref[...], w_ref[...], preferred_element_type=f32)
    acc = acc * jnp.where(pl.program_id(1) < n_scaled_tiles, scale, 1.0).astype(f32)
    tn = acc.shape[1]
    cos, sin = c_ref[...], s_ref[...]
    o_raw_ref[...] = acc.astype(o_raw_ref.dtype)
    for c in range(tn // LANES):
        seg = acc[:, c * LANES:(c + 1) * LANES]
        rot = pltpu.roll(seg, B_HEAD_DIM // 2, 1)
        o_rot_ref[:, c * LANES:(c + 1) * LANES] = (seg * cos + rot * sin).astype(o_rot_ref.dtype)


def _proj(xn, w, *, tm, tn, mode, tables=None, n_scaled_tiles=0, scale=1.0):
    n, d = xn.shape
    width = w.shape[1]
    x_spec = pl.BlockSpec((tm, d), lambda i, j: (i, 0))
    w_spec = pl.BlockSpec((d, tn), lambda i, j: (0, j))
    o_spec = pl.BlockSpec((tm, tn), lambda i, j: (i, j))
    t_spec = pl.BlockSpec((tm, LANES), lambda i, j: (i, 0))
    out = jax.ShapeDtypeStruct((n, width), bf16)
    common = dict(grid=(n // tm, width // tn),
                  compiler_params=_params(("parallel", "arbitrary")))
    if mode == "plain":
        return pl.pallas_call(_proj_plain_kernel, out_shape=out, in_specs=[x_spec, w_spec],
                              out_specs=o_spec, name="proj_plain", **common)(xn, w)
    if mode == "rope64":
        return pl.pallas_call(_proj_rope64_kernel, out_shape=out,
                              in_specs=[x_spec, w_spec, t_spec, t_spec],
                              out_specs=o_spec, name="proj_rope64", **common)(xn, w, *tables)
    kern = functools.partial(_proj_rope128_kernel, n_scaled_tiles=n_scaled_tiles, scale=scale)
    return pl.pallas_call(kern, out_shape=(out, out),
                          in_specs=[x_spec, w_spec, t_spec, t_spec],
                          out_specs=(o_spec, o_spec), name="proj_rope128", **common)(xn, w, *tables)


def _banded_kernel(sink_ref, q_ref, k_ref, v_ref, o_ref, *, groups, rep, dim, window, prev,
                   tq, use_sink, q_scale):
    q0 = pl.program_id(1) * tq
    slab = prev + tq
    pack = LANES // dim
    cols = rep // pack
    start = pl.multiple_of(jnp.maximum(q0 - prev, 0), tq)
    kpos = start + lax.broadcasted_iota(jnp.int32, (slab, tq), 0)
    tpos = q0 + lax.broadcasted_iota(jnp.int32, (slab, tq), 1)
    diff = tpos - kpos
    bias = _lane_tile(jnp.where((diff >= 0) & (diff < window), 0.0, -jnp.inf), cols)
    zeros = jnp.zeros((slab, LANES - dim), bf16)
    for g in range(groups):
        q = jnp.concatenate([q_ref[:, (g * cols + c) * LANES:(g * cols + c + 1) * LANES]
                             for c in range(cols)], axis=0)
        if q_scale != 1.0:
            q = q * q_scale
        kg = k_ref[pl.ds(start, slab), g * dim:(g + 1) * dim]
        vg = v_ref[pl.ds(start, slab), g * dim:(g + 1) * dim]
        out_t = None
        for sub in range(pack):
            if pack == 1:
                kk, vv = kg, vg
            else:
                parts = lambda a: [a, zeros] if sub == 0 else [zeros, a]
                kk, vv = jnp.concatenate(parts(kg), axis=1), jnp.concatenate(parts(vg), axis=1)
            s = lax.dot_general(kk, q, _NT, preferred_element_type=f32) + bias
            m = jnp.max(s, axis=0, keepdims=True)
            if use_sink:
                sk = jnp.concatenate([jnp.full((1, tq), sink_ref[g * rep + pack * c + sub], f32)
                                      for c in range(cols)], axis=1)
                m = jnp.maximum(m, sk)
            e = jnp.exp(s - m)
            den = jnp.sum(e, axis=0, keepdims=True)
            if use_sink:
                den = den + jnp.exp(sk - m)
            o_t = lax.dot_general(vv, e.astype(bf16), _TN, preferred_element_type=f32)
            o_t = o_t * (1.0 / den)
            out_t = o_t if out_t is None else out_t + o_t
        for c in range(cols):
            o_ref[:, (g * cols + c) * LANES:(g * cols + c + 1) * LANES] = (
                out_t[:, c * tq:(c + 1) * tq].T.astype(o_ref.dtype))


def _banded(q, k, v, sinks, *, batch, seq, q_col, k_col, v_col, groups, rep, dim, window,
            use_sink, q_scale=1.0):
    tq = BAND_BLOCK
    prev = -(-(window - 1) // tq) * tq
    nq = seq // tq
    hw, gw = groups * rep * dim, groups * dim
    assert LANES % dim == 0 and rep % (LANES // dim) == 0
    kern = functools.partial(_banded_kernel, groups=groups, rep=rep, dim=dim, window=window,
                             prev=prev, tq=tq, use_sink=use_sink, q_scale=q_scale)
    return pl.pallas_call(
        kern,
        out_shape=jax.ShapeDtypeStruct((batch * seq, hw), bf16),
        grid=(batch, nq),
        in_specs=[
            pl.BlockSpec(memory_space=pltpu.SMEM),
            pl.BlockSpec((tq, hw), lambda b, i: (b * nq + i, q_col)),
            pl.BlockSpec((seq, gw), lambda b, i: (b, k_col)),
            pl.BlockSpec((seq, gw), lambda b, i: (b, v_col)),
        ],
        out_specs=pl.BlockSpec((tq, hw), lambda b, i: (b * nq + i, 0)),
        compiler_params=_params(("parallel", "arbitrary")),
        name="banded_sink" if use_sink else "banded",
    )(sinks, q, k, v)


def _gelu_tanh(x):
    return x * (0.5 * (1.0 + jnp.tanh(np.sqrt(2.0 / np.pi).astype(np.float32)
                                      * (x + 0.044715 * (x * x * x)))))


def _compress_kernel(r_ref, pek_ref, wk1_ref, wk2_ref, pev_ref, wv1_ref, wv2_ref,
                     kc_ref, vc_ref, *, groups, dim, n_real):
    ncp = r_ref.shape[0]
    tok_w = 2 * groups * dim
    halves = CMP_BLOCK // CMP_STRIDE
    row = lax.broadcasted_iota(jnp.int32, (ncp, dim), 0)
    for which, (pe_ref, w1_ref, w2_ref, out_ref) in enumerate(
            ((pek_ref, wk1_ref, wk2_ref, kc_ref), (pev_ref, wv1_ref, wv2_ref, vc_ref))):
        for g in range(groups):
            h1 = None
            for half in range(halves):
                pieces = []
                for tl in range(CMP_STRIDE):
                    l = half * CMP_STRIDE + tl
                    col = tl * tok_w + which * groups * dim + g * dim
                    pieces.append((r_ref[:, col:col + dim].astype(f32) + pe_ref[l:l + 1, :]).astype(bf16))
                x = jnp.concatenate(pieces, axis=1)
                w = w1_ref[half * CMP_STRIDE * dim:(half + 1) * CMP_STRIDE * dim, :]
                part = jnp.dot(x, w, preferred_element_type=f32)
                if half:
                    part = pltpu.roll(part, ncp - half, 0)
                h1 = part if h1 is None else h1 + part
            y = _gelu_tanh(h1).astype(bf16)
            out = jnp.dot(y, w2_ref[...], preferred_element_type=f32)
            out_ref[g] = jnp.where(row < n_real, out, 0.0).astype(out_ref.dtype)


def _compress(r, pe_k, wk1, wk2, pe_v, wv1, wv2, *, batch, groups, dim, n_real):
    _, ncp, width = r.shape
    hidden = wk1.shape[1]
    full = lambda shape: pl.BlockSpec(shape, lambda b: (0,) * len(shape))
    out = jax.ShapeDtypeStruct((batch, groups, ncp, dim), bf16)
    o_spec = pl.BlockSpec((None, groups, ncp, dim), lambda b: (b, 0, 0, 0))
    kern = functools.partial(_compress_kernel, groups=groups, dim=dim, n_real=n_real)
    return pl.pallas_call(
        kern,
        out_shape=(out, out),
        grid=(batch,),
        in_specs=[pl.BlockSpec((None, ncp, width), lambda b: (b, 0, 0)),
                  full((CMP_BLOCK, dim)), full((CMP_BLOCK * dim, hidden)), full((hidden, dim)),
                  full((CMP_BLOCK, dim)), full((CMP_BLOCK * dim, hidden)), full((hidden, dim))],
        out_specs=(o_spec, o_spec),
        compiler_params=_params(("parallel",)),
        name="compress",
    )(r, pe_k, wk1, wk2, pe_v, wv1, wv2)


def _cmp_kernel(q_ref, kc_ref, vc_ref, selt_ref, o_ref, mb_ref, *, rep, dim, tq, n_sel):
    q0 = pl.program_id(2) * tq
    ncp = kc_ref.shape[0]
    nsp = selt_ref.shape[0]
    q = jnp.concatenate([q_ref[:, r * dim:(r + 1) * dim] for r in range(rep)], axis=0)
    cidx = lax.broadcasted_iota(jnp.int32, (ncp, tq), 0)
    tpos = q0 + lax.broadcasted_iota(jnp.int32, (ncp, tq), 1)
    bias = _lane_tile(jnp.where(cidx * CMP_STRIDE + (CMP_BLOCK - 1) <= tpos, 0.0, -jnp.inf), rep)
    s = lax.dot_general(kc_ref[...], q, _NT, preferred_element_type=f32) + bias
    m = jnp.max(s, axis=0, keepdims=True)
    m = jnp.where(m > -jnp.inf, m, 0.0)
    e = jnp.exp(s - m)
    den = jnp.sum(e, axis=0, keepdims=True)
    p = e * (1.0 / jnp.where(den > 0, den, 1.0))
    o_t = lax.dot_general(vc_ref[...], p.astype(bf16), _TN, preferred_element_type=f32)
    psum = None
    for r in range(rep):
        o_ref[:, r * dim:(r + 1) * dim] = o_t[:, r * tq:(r + 1) * tq].T.astype(o_ref.dtype)
        pr = p[:, r * tq:(r + 1) * tq]
        psum = pr if psum is None else psum + pr

    imp = jnp.dot(selt_ref[...], psum, preferred_element_type=f32, precision=lax.Precision.HIGHEST)
    j = lax.broadcasted_iota(jnp.int32, (nsp, tq), 0)
    tt = q0 + lax.broadcasted_iota(jnp.int32, (nsp, tq), 1)
    cur = tt // SLC_BLOCK
    forced = (j == 0) | (j == cur) | (j == cur - 1)
    valid = j * SLC_BLOCK <= tt
    score = jnp.where(forced, jnp.inf, jnp.where(valid, imp, -jnp.inf))
    chosen = jnp.zeros((nsp, tq), f32)
    jf = j.astype(f32)
    for _ in range(n_sel):
        best = jnp.max(score, axis=0, keepdims=True)
        first = jnp.min(jnp.where(score == best, jf, float(nsp)), axis=0, keepdims=True)
        hit = jf == first
        chosen = jnp.where(hit, 1.0, chosen)
        score = jnp.where(hit, -jnp.inf, score)
    mb_ref[...] = jnp.where(chosen > 0, 0.0, MASKED).T.astype(mb_ref.dtype)


def _cmp_attention(q_raw, kcmp, vcmp, sel_t, *, batch, seq, groups, rep, dim, tq, n_sel):
    nq = seq // tq
    ncp = kcmp.shape[2]
    nsp = sel_t.shape[0]
    kern = functools.partial(_cmp_kernel, rep=rep, dim=dim, tq=tq, n_sel=n_sel)
    kv_spec = pl.BlockSpec((None, None, ncp, dim), lambda b, g, i: (b, g, 0, 0))
    return pl.pallas_call(
        kern,
        out_shape=(jax.ShapeDtypeStruct((batch * seq, groups * rep * dim), bf16),
                   jax.ShapeDtypeStruct((batch, groups, seq, nsp), bf16)),
        grid=(batch, groups, nq),
        in_specs=[pl.BlockSpec((tq, rep * dim), lambda b, g, i: (b * nq + i, g)),
                  kv_spec, kv_spec,
                  pl.BlockSpec((nsp, ncp), lambda b, g, i: (0, 0))],
        out_specs=(pl.BlockSpec((tq, rep * dim), lambda b, g, i: (b * nq + i, g)),
                   pl.BlockSpec((None, None, tq, nsp), lambda b, g, i: (b, g, i, 0))),
        compiler_params=_params(("parallel", "parallel", "arbitrary")),
        name="cmp_attention",
    )(q_raw, kcmp, vcmp, sel_t)


def _slc_kernel(q_ref, mb_ref, k_ref, v_ref, o_ref, kp_ref, qp_ref, *, groups, rep, dim, tq, kt):
    i = pl.program_id(1)
    seq = k_ref.shape[0]
    nsp = mb_ref.shape[-1]
    rows = rep * tq

    @pl.when(i == 0)
    def _():
        blk = lax.broadcasted_iota(jnp.int32, (seq, nsp), 0) // SLC_BLOCK
        lane = lax.broadcasted_iota(jnp.int32, (seq, nsp), 1)
        onehot = jnp.where(blk == lane, 1.0, 0.0).astype(bf16)
        for g in range(groups):
            kp_ref[g, :, :dim] = k_ref[:, g * dim:(g + 1) * dim]
            kp_ref[g, :, dim:] = onehot

    for g in range(groups):
        mb = mb_ref[g]
        for r in range(rep):
            h = g * rep + r
            qp_ref[g, r * tq:(r + 1) * tq, :dim] = q_ref[:, h * dim:(h + 1) * dim]
            qp_ref[g, r * tq:(r + 1) * tq, dim:] = mb

    def update(g, carry, k0, bias):
        m, l, acc = carry
        s = lax.dot_general(kp_ref[g, pl.ds(k0, kt), :], qp_ref[g], _NT, preferred_element_type=f32)
        if bias is not None:
            s = s + bias
        m_new = jnp.maximum(m, jnp.max(s, axis=0, keepdims=True))
        alpha = jnp.exp(m - m_new)
        p = jnp.exp(s - m_new)
        l = alpha * l + jnp.sum(p, axis=0, keepdims=True)
        acc = alpha * acc + lax.dot_general(v_ref[pl.ds(k0, kt), g * dim:(g + 1) * dim], p.astype(bf16),
                                            _TN, preferred_element_type=f32)
        return m_new, l, acc

    def body(t, carries):
        k0 = pl.multiple_of(t * kt, kt)
        return tuple(update(g, carries[g], k0, None) for g in range(groups))

    n_full = (i * tq) // kt
    init = (jnp.full((1, rows), M_INIT, f32), jnp.zeros((1, rows), f32), jnp.zeros((dim, rows), f32))
    carries = lax.fori_loop(0, n_full, body, (init,) * groups)
    k0 = pl.multiple_of(n_full * kt, kt)
    kpos = k0 + lax.broadcasted_iota(jnp.int32, (kt, tq), 0)
    tpos = i * tq + lax.broadcasted_iota(jnp.int32, (kt, tq), 1)
    causal = _lane_tile(jnp.where(kpos <= tpos, 0.0, MASKED), rep)
    for g in range(groups):
        _, l, acc = update(g, carries[g], k0, causal)
        o_t = acc * (1.0 / l)
        for r in range(rep):
            h = g * rep + r
            o_ref[:, h * dim:(h + 1) * dim] = o_t[:, r * tq:(r + 1) * tq].T.astype(o_ref.dtype)


def _slc_attention(q_rot, mbias, k_rot, v, *, batch, seq, groups, rep, dim, tq, kt,
                   k_col, v_col):
    nq = seq // tq
    nsp = mbias.shape[-1]
    hw, gw = groups * rep * dim, groups * dim
    kern = functools.partial(_slc_kernel, groups=groups, rep=rep, dim=dim, tq=tq, kt=kt)
    return pl.pallas_call(
        kern,
        out_shape=jax.ShapeDtypeStruct((batch * seq, hw), bf16),
        grid=(batch, nq),
        in_specs=[pl.BlockSpec((tq, hw), lambda b, i: (b * nq + i, 0)),
                  pl.BlockSpec((None, groups, tq, nsp), lambda b, i: (b, 0, i, 0)),
                  pl.BlockSpec((seq, gw), lambda b, i: (b, k_col)),
                  pl.BlockSpec((seq, gw), lambda b, i: (b, v_col))],
        out_specs=pl.BlockSpec((tq, hw), lambda b, i: (b * nq + i, 0)),
        scratch_shapes=[pltpu.VMEM((groups, seq, dim + nsp), bf16),
                        pltpu.VMEM((groups, rep * tq, dim + nsp), bf16)],
        compiler_params=_params(("parallel", "arbitrary")),
        name="slc_attention",
    )(q_rot, mbias, k_rot, v)


def _merge_kernel(oa_ref, oc_ref, os_ref, ow_ref, gb_ref, ga_ref, gbm_ref, woa_ref, wob_ref,
                  o_ref, ob_ref, *, heads, dim):
    @pl.when(pl.program_id(1) == 0)
    def _():
        g = _sigmoid(gb_ref[...].astype(f32))
        for h in range(heads):
            cols = slice(h * dim, (h + 1) * dim)
            ob = (g[:, 3 * h:3 * h + 1] * oc_ref[:, cols].astype(f32)
                  + g[:, 3 * h + 1:3 * h + 2] * os_ref[:, cols].astype(f32)
                  + g[:, 3 * h + 2:3 * h + 3] * ow_ref[:, cols].astype(f32))
            ob_ref[:, cols] = ob.astype(bf16)

    ya = jnp.dot(oa_ref[...], woa_ref[...], preferred_element_type=f32)
    yb = jnp.dot(ob_ref[...], wob_ref[...], preferred_element_type=f32)
    merged = _sigmoid(ga_ref[...].astype(f32)) * ya + _sigmoid(gbm_ref[...].astype(f32)) * yb
    o_ref[...] = merged.astype(o_ref.dtype)


def _merge(o_a, o_cmp, o_slc, o_win, proj_c, w_oa, w_ob, *, tm, tn, gb_col, ga_col, gbm_col):
    n, da = o_a.shape
    db = o_cmp.shape[1]
    d = w_oa.shape[1]
    row = lambda w: pl.BlockSpec((tm, w), lambda i, j: (i, 0))
    kern = functools.partial(_merge_kernel, heads=B_HEADS, dim=B_HEAD_DIM)
    return pl.pallas_call(
        kern,
        out_shape=jax.ShapeDtypeStruct((n, d), bf16),
        grid=(n // tm, d // tn),
        in_specs=[row(da), row(db), row(db), row(db),
                  pl.BlockSpec((tm, LANES), lambda i, j: (i, gb_col)),
                  pl.BlockSpec((tm, tn), lambda i, j: (i, ga_col + j)),
                  pl.BlockSpec((tm, tn), lambda i, j: (i, gbm_col + j)),
                  pl.BlockSpec((da, tn), lambda i, j: (0, j)),
                  pl.BlockSpec((db, tn), lambda i, j: (0, j))],
        out_specs=pl.BlockSpec((tm, tn), lambda i, j: (i, j)),
        scratch_shapes=[pltpu.VMEM((tm, db), bf16)],
        compiler_params=_params(("parallel", "arbitrary")),
        name="merge",
    )(o_a, o_cmp, o_slc, o_win, proj_c, proj_c, proj_c, w_oa, w_ob)


def _matmul_res_kernel(x_ref, w_ref, r_ref, o_ref):
    o_ref[...] = r_ref[...] + jnp.dot(x_ref[...], w_ref[...], preferred_element_type=f32)


def _matmul_res(x, w, res, *, tm, tn):
    n, k = x.shape
    d = w.shape[1]
    return pl.pallas_call(
        _matmul_res_kernel,
        out_shape=jax.ShapeDtypeStruct((n, d), f32),
        grid=(n // tm, d // tn),
        in_specs=[pl.BlockSpec((tm, k), lambda i, j: (i, 0)),
                  pl.BlockSpec((k, tn), lambda i, j: (0, j)),
                  pl.BlockSpec((tm, tn), lambda i, j: (i, j))],
        out_specs=pl.BlockSpec((tm, tn), lambda i, j: (i, j)),
        compiler_params=_params(("parallel", "arbitrary")),
        name="matmul_res",
    )(x, w, res)


def _ple_kernel(xn_ref, h_ref, p_ref, wg_ref, wp_ref, fn_ref, o_ref, h4_ref):
    j = pl.program_id(1)
    nj = pl.num_programs(1)
    tn = h_ref.shape[1]
    gate = _sigmoid(jnp.dot(xn_ref[...], wg_ref[...], preferred_element_type=f32))
    emb = jnp.dot(p_ref[...].astype(bf16), wp_ref[...], preferred_element_type=f32)
    h4_ref[j] = h_ref[...] + emb * gate

    @pl.when(j == nj - 1)
    def _():
        n_tiles = h4_ref.shape[0]
        ss = None
        for c in range(n_tiles):
            t = h4_ref[c]
            part = jnp.sum(t * t, axis=-1, keepdims=True)
            ss = part if ss is None else ss + part
        inv = lax.rsqrt(ss / (n_tiles * tn) + EPS)
        for c in range(n_tiles):
            o_ref[:, c * tn:(c + 1) * tn] = h4_ref[c] * inv * fn_ref[:, c * tn:(c + 1) * tn]


def _ple(xn, h, p, w_gate, w_proj, final_norm, *, tm, tn):
    n, d = h.shape
    pd = p.shape[1]
    return pl.pallas_call(
        _ple_kernel,
        out_shape=jax.ShapeDtypeStruct((n, d), f32),
        grid=(n // tm, d // tn),
        in_specs=[pl.BlockSpec((tm, d), lambda i, j: (i, 0)),
                  pl.BlockSpec((tm, tn), lambda i, j: (i, j)),
                  pl.BlockSpec((tm, pd), lambda i, j: (i, 0)),
                  pl.BlockSpec((d, tn), lambda i, j: (0, j)),
                  pl.BlockSpec((pd, tn), lambda i, j: (0, j)),
                  pl.BlockSpec((1, d), lambda i, j: (0, 0))],
        out_specs=pl.BlockSpec((tm, d), lambda i, j: (i, 0)),
        scratch_shapes=[pltpu.VMEM((d // tn, tm, tn), f32)],
        compiler_params=_params(("parallel", "arbitrary")),
        name="ple_final",
    )(xn, h, p, w_gate, w_proj, final_norm.reshape(1, d))


def _selection_map_t(nc, ns, ncp, nsp):
    a, b = SLC_BLOCK // CMP_STRIDE, CMP_BLOCK // CMP_STRIDE
    j = np.arange(ns)[:, None, None]
    c = a * j - np.arange(a)[None, :, None] - np.arange(b)[None, None, :]
    jj = np.broadcast_to(j, c.shape)
    ok = (c >= 0) & (c < nc)
    mat = np.zeros((nsp, ncp), np.float32)
    np.add.at(mat, (jj[ok], c[ok]), 1.0)
    return jnp.asarray(mat)


def _split_w_in(w_in):
    qa_w, kv_a = A_HEADS * A_HEAD_DIM, A_KV_HEADS * A_HEAD_DIM
    qb_w, kv_b = B_HEADS * B_HEAD_DIM, B_KV_GROUPS * B_HEAD_DIM
    d_model = w_in.shape[0]
    sizes = [qa_w, kv_a, kv_a, qb_w] + [kv_b] * 6 + [B_HEADS * 3, d_model, d_model]
    offs = np.concatenate([[0], np.cumsum(sizes)])
    qa, ka, va, qb, kc, vc, ksl, vsl, kwn, vwn, gb, ga, gbm = [
        w_in[:, int(offs[i]):int(offs[i + 1])] for i in range(len(sizes))]
    gb_pad = jnp.zeros((d_model, 2 * LANES - B_HEADS * 3), w_in.dtype)
    w_a = jnp.concatenate([qa, ka], axis=1).astype(bf16)
    w_b = jnp.concatenate([qb, ksl, kwn], axis=1).astype(bf16)
    w_c = jnp.concatenate([ga, gbm, va, kc, vc, vsl, vwn, gb, gb_pad], axis=1).astype(bf16)
    return w_a, w_b, w_c


def _layer(h, tabs, w, *, batch, seq):
    n, d_model = h.shape
    c64, s64, c128, s128 = tabs
    kv_a = A_KV_HEADS * A_HEAD_DIM
    kv_b = B_KV_GROUPS * B_HEAD_DIM
    qb_w = B_HEADS * B_HEAD_DIM
    rep_b = B_HEADS // B_KV_GROUPS

    h = _ffn(h, w["ffn1_norm"], w["ffn1_w_gate"], w["ffn1_w_up"], w["ffn1_w_down"], tm=512)

    xn = _norm_cast(h, w["mix_norm"], tm=512)
    w_a, w_b, w_c = w["w_in_split"]
    proj_a = _proj(xn, w_a, tm=1024, tn=768, mode="rope64", tables=(c64, s64))
    proj_b_rot, proj_b_raw = _proj(xn, w_b, tm=1024, tn=512, mode="rope128", tables=(c128, s128),
                                   n_scaled_tiles=qb_w // 512, scale=B_HEAD_DIM ** -0.5)
    c0 = 2 * d_model
    proj_c = _proj(xn, w_c, tm=1024, tn=512, mode="plain")

    o_a = _banded(proj_a, proj_a, proj_c, w["sinks"], batch=batch, seq=seq,
                  q_col=0, k_col=(A_HEADS * A_HEAD_DIM) // kv_a, v_col=c0 // kv_a,
                  groups=A_KV_HEADS, rep=A_HEADS // A_KV_HEADS, dim=A_HEAD_DIM,
                  window=A_WINDOW, use_sink=True, q_scale=A_HEAD_DIM ** -0.5)

    ncp = seq // CMP_STRIDE
    nc = (seq - CMP_BLOCK) // CMP_STRIDE + 1
    ns = seq // SLC_BLOCK
    kvc = proj_c[:, c0 + kv_a:c0 + kv_a + 2 * kv_b].reshape(batch, ncp, CMP_STRIDE * 2 * kv_b)
    kcmp, vcmp = _compress(kvc, w["nsa_pe_k"], w["nsa_w_ck1"], w["nsa_w_ck2"],
                           w["nsa_pe_v"], w["nsa_w_cv1"], w["nsa_w_cv2"],
                           batch=batch, groups=B_KV_GROUPS, dim=B_HEAD_DIM, n_real=nc)
    sel_t = _selection_map_t(nc, ns, ncp, LANES)
    o_cmp, mbias = _cmp_attention(proj_b_raw, kcmp, vcmp, sel_t, batch=batch, seq=seq,
                                  groups=B_KV_GROUPS, rep=rep_b, dim=B_HEAD_DIM, tq=128,
                                  n_sel=min(N_SELECT, ns))
    o_slc = _slc_attention(proj_b_rot, mbias, proj_b_rot, proj_c, batch=batch, seq=seq,
                           groups=B_KV_GROUPS, rep=rep_b, dim=B_HEAD_DIM, tq=256,
                           kt=min(512, seq),
                           k_col=qb_w // kv_b, v_col=(c0 + kv_a + 2 * kv_b) // kv_b)
    o_win = _banded(proj_b_rot, proj_b_rot, proj_c, w["sinks"], batch=batch, seq=seq,
                    q_col=0, k_col=(qb_w + kv_b) // kv_b, v_col=(c0 + kv_a + 3 * kv_b) // kv_b,
                    groups=B_KV_GROUPS, rep=rep_b, dim=B_HEAD_DIM, window=B_WINDOW,
                    use_sink=False)

    tn = 1024
    merged = _merge(o_a, o_cmp, o_slc, o_win, proj_c, w["w_o_a"], w["w_o_b"], tm=512, tn=tn,
                    gb_col=(c0 + kv_a + 4 * kv_b) // LANES, ga_col=0, gbm_col=d_model // tn)
    h = _matmul_res(merged, w["w_o"], h, tm=512, tn=1024)

    h = _ffn(h, w["ffn2_norm"], w["ffn2_w_gate"], w["ffn2_w_up"], w["ffn2_w_down"], tm=512)
    return h


def kernel(x, p, positions, ffn1_norm, ffn1_w_gate, ffn1_w_up, ffn1_w_down, mix_norm, w_in, sinks, nsa_pe_k, nsa_w_ck1, nsa_w_ck2, nsa_pe_v, nsa_w_cv1, nsa_w_cv2, w_o_a, w_o_b, w_o, ffn2_norm, ffn2_w_gate, ffn2_w_up, ffn2_w_down, ple_norm, w_ple_gate, w_ple_proj, final_norm):
    batch, seq, d_model = x.shape
    depth = ffn1_norm.shape[0]
    assert depth == 1, "the final norm is fused into the last layer's embedding step"
    assert seq // SLC_BLOCK <= LANES and seq % 512 == 0
    n = batch * seq
    h = x.reshape(n, d_model)
    tabs = _rope_tables(positions, tm=1024)
    i = 0
    small = lambda a: a[i].astype(bf16)
    rows = lambda a: _cast_bf16(a[i], bk=min(FFN_TILE, a.shape[1]), bn=a.shape[2])
    tiles = lambda a: _cast_bf16(a[i], bk=a.shape[1], bn=FFN_TILE, tiled=True)
    w = dict(
        ffn1_norm=ffn1_norm[i], ffn1_w_gate=tiles(ffn1_w_gate), ffn1_w_up=tiles(ffn1_w_up),
        ffn1_w_down=rows(ffn1_w_down), mix_norm=mix_norm[i], w_in_split=_split_w_in(w_in[i]),
        sinks=sinks[i], nsa_pe_k=nsa_pe_k[i], nsa_w_ck1=small(nsa_w_ck1), nsa_w_ck2=small(nsa_w_ck2),
        nsa_pe_v=nsa_pe_v[i], nsa_w_cv1=small(nsa_w_cv1), nsa_w_cv2=small(nsa_w_cv2),
        w_o_a=rows(w_o_a), w_o_b=rows(w_o_b), w_o=rows(w_o),
        ffn2_norm=ffn2_norm[i], ffn2_w_gate=tiles(ffn2_w_gate), ffn2_w_up=tiles(ffn2_w_up),
        ffn2_w_down=rows(ffn2_w_down))
    h = _layer(h, tabs, w, batch=batch, seq=seq)
    out = _ple(_norm_cast(h, ple_norm[i], tm=512), h, p[i].reshape(n, -1), rows(w_ple_gate),
               rows(w_ple_proj), final_norm, tm=512, tn=512)
    return out.reshape(batch, seq, d_model)
```

```python
import functools

import numpy as np
import jax
import jax.numpy as jnp
from jax import lax
from jax.experimental import pallas as pl
from jax.experimental.pallas import tpu as pltpu

f32 = jnp.float32
bf16 = jnp.bfloat16

A_HEADS, A_KV_HEADS, A_HEAD_DIM, A_WINDOW = 32, 4, 64, 128
B_HEADS, B_KV_GROUPS, B_HEAD_DIM = 16, 2, 128
CMP_BLOCK, CMP_STRIDE = 32, 16
SLC_BLOCK, N_SELECT, B_WINDOW = 64, 16, 512
BAND_BLOCK = 128
ROPE_THETA = 10000.0
EPS = 1e-6

LANES = 128
VMEM_LIMIT = 56 * 1024 * 1024
VMEM_LIMIT_FFN = 60 * 1024 * 1024
FFN_TILE = 512
FFN_DOWN_CHUNK = 1024
FFN_NORM_ROWS = 128
MASKED = -1e30
M_INIT = -1e29

_NT = (((1,), (1,)), ((), ()))
_TN = (((0,), (0,)), ((), ()))


def _params(sem, limit=VMEM_LIMIT):
    return pltpu.CompilerParams(dimension_semantics=sem, vmem_limit_bytes=limit)


def _rmsnorm(x, g):
    var = jnp.mean(x * x, axis=-1, keepdims=True)
    return x * lax.rsqrt(var + EPS) * g


def _sigmoid(x):
    return 1.0 / (1.0 + jnp.exp(-x))


def _lane_tile(x, n):
    return x if n == 1 else jnp.concatenate([x] * n, axis=1)


def _cast_kernel(w_ref, o_ref, *, k, n):
    bk, bn = w_ref.shape
    w = w_ref[...]
    if k % bk or n % bn:
        row = pl.program_id(0) * bk + lax.broadcasted_iota(jnp.int32, (bk, bn), 0)
        col = pl.program_id(1) * bn + lax.broadcasted_iota(jnp.int32, (bk, bn), 1)
        w = jnp.where((row < k) & (col < n), w, 0.0)
    o_ref[...] = w.astype(o_ref.dtype)


def _cast_bf16(w, *, bk, bn, tiled=False):
    k, n = w.shape
    gk, gn = pl.cdiv(k, bk), pl.cdiv(n, bn)
    if tiled:
        out = jax.ShapeDtypeStruct((gn, gk * bk, bn), bf16)
        o_spec = pl.BlockSpec((None, bk, bn), lambda i, j: (j, i, 0))
    else:
        out = jax.ShapeDtypeStruct((gk * bk, gn * bn), bf16)
        o_spec = pl.BlockSpec((bk, bn), lambda i, j: (i, j))
    return pl.pallas_call(
        functools.partial(_cast_kernel, k=k, n=n),
        out_shape=out,
        grid=(gk, gn),
        in_specs=[pl.BlockSpec((bk, bn), lambda i, j: (i, j))],
        out_specs=o_spec,
        compiler_params=_params(("parallel", "parallel")),
        name="cast_bf16",
    )(w)


def _ffn_kernel(x_ref, g_ref, wg_ref, wu_ref, wd_ref, o_ref, xn_ref, *, tail):
    j = pl.program_id(1)

    @pl.when(j == 0)
    def _():
        for c in range(x_ref.shape[0] // FFN_NORM_ROWS):
            rows = slice(c * FFN_NORM_ROWS, (c + 1) * FFN_NORM_ROWS)
            x = x_ref[rows, :]
            xn_ref[rows, :] = _rmsnorm(x, g_ref[...]).astype(bf16)
            o_ref[rows, :] = x

    def step(width):
        xn = xn_ref[...]
        gate = jnp.dot(xn, wg_ref[:, :width], preferred_element_type=f32)
        up = jnp.dot(xn, wu_ref[:, :width], preferred_element_type=f32)
        act = (0.5 * gate * _sigmoid(gate) * up).astype(bf16)
        d = o_ref.shape[1]
        for c in range(d // FFN_DOWN_CHUNK):
            cols = slice(c * FFN_DOWN_CHUNK, (c + 1) * FFN_DOWN_CHUNK)
            o_ref[:, cols] += jnp.dot(act, wd_ref[:width, cols], preferred_element_type=f32)

    tf = wg_ref.shape[1]
    if tail == tf:
        step(tf)
    else:
        last = pl.num_programs(1) - 1
        pl.when(j != last)(lambda: step(tf))
        pl.when(j == last)(lambda: step(tail))


def _ffn(x, g, wg, wu, wd, *, tm, ff):
    n, d = x.shape
    n_tiles, _, tf = wg.shape
    once = dict(pipeline_mode=pl.Buffered(1))
    return pl.pallas_call(
        functools.partial(_ffn_kernel, tail=ff - (n_tiles - 1) * tf),
        out_shape=jax.ShapeDtypeStruct((n, d), f32),
        grid=(n // tm, n_tiles),
        in_specs=[
            pl.BlockSpec((tm, d), lambda i, j: (i, 0), **once),
            pl.BlockSpec((1, d), lambda i, j: (0, 0)),
            pl.BlockSpec((None, d, tf), lambda i, j: (j, 0, 0)),
            pl.BlockSpec((None, d, tf), lambda i, j: (j, 0, 0)),
            pl.BlockSpec((tf, d), lambda i, j: (j, 0)),
        ],
        out_specs=pl.BlockSpec((tm, d), lambda i, j: (i, 0)),
        scratch_shapes=[pltpu.VMEM((tm, d), bf16)],
        compiler_params=_params(("parallel", "arbitrary"), VMEM_LIMIT_FFN),
        name="ffn",
    )(x, g.reshape(1, d), wg, wu, wd)


def _norm_cast_kernel(x_ref, g_ref, o_ref):
    o_ref[...] = _rmsnorm(x_ref[...], g_ref[...]).astype(bf16)


def _norm_cast(x, g, *, tm):
    n, d = x.shape
    return pl.pallas_call(
        _norm_cast_kernel,
        out_shape=jax.ShapeDtypeStruct((n, d), bf16),
        grid=(n // tm,),
        in_specs=[pl.BlockSpec((tm, d), lambda i: (i, 0)),
                  pl.BlockSpec((1, d), lambda i: (0, 0))],
        out_specs=pl.BlockSpec((tm, d), lambda i: (i, 0)),
        compiler_params=_params(("parallel",)),
        name="norm_cast",
    )(x, g.reshape(1, d))


def _rope_table_kernel(pos_ref, inv64_ref, inv128_ref, c64_ref, s64_ref, c128_ref, s128_ref):
    pos = pos_ref[...].astype(f32)
    lane = lax.broadcasted_iota(jnp.int32, c64_ref.shape, 1)
    a64 = pos * inv64_ref[...]
    a128 = pos * inv128_ref[...]
    c64_ref[...] = jnp.cos(a64)
    s64_ref[...] = jnp.where(lane % A_HEAD_DIM < A_HEAD_DIM // 2, -1.0, 1.0) * jnp.sin(a64)
    c128_ref[...] = jnp.cos(a128)
    s128_ref[...] = jnp.where(lane < B_HEAD_DIM // 2, -1.0, 1.0) * jnp.sin(a128)


def _rope_tables(positions, *, tm):
    n = positions.size

    def inv(d):
        v = jnp.power(jnp.float32(ROPE_THETA), -jnp.arange(0, d, 2, dtype=f32) / d)
        return jnp.tile(v, LANES // (d // 2)).reshape(1, LANES)

    tab = jax.ShapeDtypeStruct((n, LANES), f32)
    row = pl.BlockSpec((1, LANES), lambda i: (0, 0))
    blk = pl.BlockSpec((tm, LANES), lambda i: (i, 0))
    return pl.pallas_call(
        _rope_table_kernel,
        out_shape=(tab, tab, tab, tab),
        grid=(n // tm,),
        in_specs=[pl.BlockSpec((tm, 1), lambda i: (i, 0)), row, row],
        out_specs=(blk, blk, blk, blk),
        compiler_params=_params(("parallel",)),
        name="rope_tables",
    )(positions.reshape(n, 1), inv(A_HEAD_DIM), inv(B_HEAD_DIM))


def _proj_plain_kernel(x_ref, w_ref, o_ref):
    o_ref[...] = jnp.dot(x_ref[...], w_ref[...], preferred_element_type=f32).astype(o_ref.dtype)


def _proj_rope64_kernel(x_ref, w_ref, c_ref, s_ref, o_ref):
    acc = jnp.dot(x_ref[...], w_ref[...], preferred_element_type=f32)
    tn = acc.shape[1]
    half = A_HEAD_DIM // 2
    first = lax.broadcasted_iota(jnp.int32, c_ref.shape, 1) % A_HEAD_DIM < half
    cos, sin = c_ref[...], s_ref[...]
    for c in range(tn // LANES):
        seg = acc[:, c * LANES:(c + 1) * LANES]
        rot = jnp.where(first, pltpu.roll(seg, LANES - half, 1), pltpu.roll(seg, half, 1))
        o_ref[:, c * LANES:(c + 1) * LANES] = (seg * cos + rot * sin).astype(o_ref.dtype)


def _proj_rope128_kernel(x_ref, w_ref, c_ref, s_ref, o_rot_ref, o_raw_ref, *, n_scaled_tiles, scale):
    acc = jnp.dot(x_ref[...], w_ref[...], preferred_element_type=f32)
    acc = acc * jnp.where(pl.program_id(1) < n_scaled_tiles, scale, 1.0).astype(f32)
    tn = acc.shape[1]
    cos, sin = c_ref[...], s_ref[...]
    o_raw_ref[...] = acc.astype(o_raw_ref.dtype)
    for c in range(tn // LANES):
        seg = acc[:, c * LANES:(c + 1) * LANES]
        rot = pltpu.roll(seg, B_HEAD_DIM // 2, 1)
        o_rot_ref[:, c * LANES:(c + 1) * LANES] = (seg * cos + rot * sin).astype(o_rot_ref.dtype)


def _proj(xn, w, *, tm, tn, mode, tables=None, n_scaled_tiles=0, scale=1.0):
    n, d = xn.shape
    width = w.shape[1]
    x_spec = pl.BlockSpec((tm, d), lambda i, j: (i, 0))
    w_spec = pl.BlockSpec((d, tn), lambda i, j: (0, j))
    o_spec = pl.BlockSpec((tm, tn), lambda i, j: (i, j))
    t_spec = pl.BlockSpec((tm, LANES), lambda i, j: (i, 0))
    out = jax.ShapeDtypeStruct((n, width), bf16)
    common = dict(grid=(n // tm, width // tn),
                  compiler_params=_params(("parallel", "arbitrary")))
    if mode == "plain":
        return pl.pallas_call(_proj_plain_kernel, out_shape=out, in_specs=[x_spec, w_spec],
                              out_specs=o_spec, name="proj_plain", **common)(xn, w)
    if mode == "rope64":
        return pl.pallas_call(_proj_rope64_kernel, out_shape=out,
                              in_specs=[x_spec, w_spec, t_spec, t_spec],
                              out_specs=o_spec, name="proj_rope64", **common)(xn, w, *tables)
    kern = functools.partial(_proj_rope128_kernel, n_scaled_tiles=n_scaled_tiles, scale=scale)
    return pl.pallas_call(kern, out_shape=(out, out),
                          in_specs=[x_spec, w_spec, t_spec, t_spec],
                          out_specs=(o_spec, o_spec), name="proj_rope128", **common)(xn, w, *tables)


def _banded_kernel(sink_ref, q_ref, k_ref, v_ref, o_ref, *, groups, rep, dim, window, prev,
                   tq, use_sink, q_scale):
    q0 = pl.program_id(1) * tq
    slab = prev + tq
    pack = LANES // dim
    cols = rep // pack
    start = pl.multiple_of(jnp.maximum(q0 - prev, 0), tq)
    kpos = start + lax.broadcasted_iota(jnp.int32, (slab, tq), 0)
    tpos = q0 + lax.broadcasted_iota(jnp.int32, (slab, tq), 1)
    diff = tpos - kpos
    bias = _lane_tile(jnp.where((diff >= 0) & (diff < window), 0.0, -jnp.inf), cols)
    zeros = jnp.zeros((slab, LANES - dim), bf16)
    for g in range(groups):
        q = jnp.concatenate([q_ref[:, (g * cols + c) * LANES:(g * cols + c + 1) * LANES]
                             for c in range(cols)], axis=0)
        if q_scale != 1.0:
            q = q * q_scale
        kg = k_ref[pl.ds(start, slab), g * dim:(g + 1) * dim]
        vg = v_ref[pl.ds(start, slab), g * dim:(g + 1) * dim]
        out_t = None
        for sub in range(pack):
            if pack == 1:
                kk, vv = kg, vg
            else:
                parts = lambda a: [a, zeros] if sub == 0 else [zeros, a]
                kk, vv = jnp.concatenate(parts(kg), axis=1), jnp.concatenate(parts(vg), axis=1)
            s = lax.dot_general(kk, q, _NT, preferred_element_type=f32) + bias
            m = jnp.max(s, axis=0, keepdims=True)
            if use_sink:
                sk = jnp.concatenate([jnp.full((1, tq), sink_ref[g * rep + pack * c + sub], f32)
                                      for c in range(cols)], axis=1)
                m = jnp.maximum(m, sk)
            e = jnp.exp(s - m)
            den = jnp.sum(e, axis=0, keepdims=True)
            if use_sink:
                den = den + jnp.exp(sk - m)
            o_t = lax.dot_general(vv, e.astype(bf16), _TN, preferred_element_type=f32)
            o_t = o_t * (1.0 / den)
            out_t = o_t if out_t is None else out_t + o_t
        for c in range(cols):
            o_ref[:, (g * cols + c) * LANES:(g * cols + c + 1) * LANES] = (
                out_t[:, c * tq:(c + 1) * tq].T.astype(o_ref.dtype))


def _banded(q, k, v, sinks, *, batch, seq, q_col, k_col, v_col, groups, rep, dim, window,
            use_sink, q_scale=1.0):
    tq = BAND_BLOCK
    prev = -(-(window - 1) // tq) * tq
    nq = seq // tq
    hw, gw = groups * rep * dim, groups * dim
    assert LANES % dim == 0 and rep % (LANES // dim) == 0
    kern = functools.partial(_banded_kernel, groups=groups, rep=rep, dim=dim, window=window,
                             prev=prev, tq=tq, use_sink=use_sink, q_scale=q_scale)
    return pl.pallas_call(
        kern,
        out_shape=jax.ShapeDtypeStruct((batch * seq, hw), bf16),
        grid=(batch, nq),
        in_specs=[
            pl.BlockSpec(memory_space=pltpu.SMEM),
            pl.BlockSpec((tq, hw), lambda b, i: (b * nq + i, q_col)),
            pl.BlockSpec((seq, gw), lambda b, i: (b, k_col)),
            pl.BlockSpec((seq, gw), lambda b, i: (b, v_col)),
        ],
        out_specs=pl.BlockSpec((tq, hw), lambda b, i: (b * nq + i, 0)),
        compiler_params=_params(("parallel", "arbitrary")),
        name="banded_sink" if use_sink else "banded",
    )(sinks, q, k, v)


def _gelu_tanh(x):
    return x * (0.5 * (1.0 + jnp.tanh(np.sqrt(2.0 / np.pi).astype(np.float32)
                                      * (x + 0.044715 * (x * x * x)))))


def _compress_kernel(r_ref, pek_ref, wk1_ref, wk2_ref, pev_ref, wv1_ref, wv2_ref,
                     kc_ref, vc_ref, *, groups, dim, n_real):
    ncp = r_ref.shape[0]
    tok_w = 2 * groups * dim
    halves = CMP_BLOCK // CMP_STRIDE
    row = lax.broadcasted_iota(jnp.int32, (ncp, dim), 0)
    for which, (pe_ref, w1_ref, w2_ref, out_ref) in enumerate(
            ((pek_ref, wk1_ref, wk2_ref, kc_ref), (pev_ref, wv1_ref, wv2_ref, vc_ref))):
        for g in range(groups):
            h1 = None
            for half in range(halves):
                pieces = []
                for tl in range(CMP_STRIDE):
                    l = half * CMP_STRIDE + tl
                    col = tl * tok_w + which * groups * dim + g * dim
                    pieces.append((r_ref[:, col:col + dim].astype(f32) + pe_ref[l:l + 1, :]).astype(bf16))
                x = jnp.concatenate(pieces, axis=1)
                w = w1_ref[half * CMP_STRIDE * dim:(half + 1) * CMP_STRIDE * dim, :]
                part = jnp.dot(x, w, preferred_element_type=f32)
                if half:
                    part = pltpu.roll(part, ncp - half, 0)
                h1 = part if h1 is None else h1 + part
            y = _gelu_tanh(h1).astype(bf16)
            out = jnp.dot(y, w2_ref[...], preferred_element_type=f32)
            out_ref[g] = jnp.where(row < n_real, out, 0.0).astype(out_ref.dtype)


def _compress(r, pe_k, wk1, wk2, pe_v, wv1, wv2, *, batch, groups, dim, n_real):
    _, ncp, width = r.shape
    hidden = wk1.shape[1]
    full = lambda shape: pl.BlockSpec(shape, lambda b: (0,) * len(shape))
    out = jax.ShapeDtypeStruct((batch, groups, ncp, dim), bf16)
    o_spec = pl.BlockSpec((None, groups, ncp, dim), lambda b: (b, 0, 0, 0))
    kern = functools.partial(_compress_kernel, groups=groups, dim=dim, n_real=n_real)
    return pl.pallas_call(
        kern,
        out_shape=(out, out),
        grid=(batch,),
        in_specs=[pl.BlockSpec((None, ncp, width), lambda b: (b, 0, 0)),
                  full((CMP_BLOCK, dim)), full((CMP_BLOCK * dim, hidden)), full((hidden, dim)),
                  full((CMP_BLOCK, dim)), full((CMP_BLOCK * dim, hidden)), full((hidden, dim))],
        out_specs=(o_spec, o_spec),
        compiler_params=_params(("parallel",)),
        name="compress",
    )(r, pe_k, wk1, wk2, pe_v, wv1, wv2)


def _cmp_kernel(q_ref, kc_ref, vc_ref, selt_ref, o_ref, mb_ref, imp_ref, *, rep, dim, tq, n_sel, chunk):
    q0 = pl.program_id(2) * tq
    ncp = kc_ref.shape[0]
    nsp = selt_ref.shape[0]
    q = jnp.concatenate([q_ref[:, r * dim:(r + 1) * dim] for r in range(rep)], axis=0)

    def attend(nk):
        cidx = lax.broadcasted_iota(jnp.int32, (nk, tq), 0)
        tpos = q0 + lax.broadcasted_iota(jnp.int32, (nk, tq), 1)
        bias = _lane_tile(jnp.where(cidx * CMP_STRIDE + (CMP_BLOCK - 1) <= tpos, 0.0, -jnp.inf), rep)
        s = lax.dot_general(kc_ref[:nk, :], q, _NT, preferred_element_type=f32) + bias
        m = jnp.max(s, axis=0, keepdims=True)
        m = jnp.where(m > -jnp.inf, m, 0.0)
        e = jnp.exp(s - m)
        den = jnp.sum(e, axis=0, keepdims=True)
        p = e * (1.0 / jnp.where(den > 0, den, 1.0))
        o_t = lax.dot_general(vc_ref[:nk, :], p.astype(bf16), _TN, preferred_element_type=f32)
        psum = None
        for r in range(rep):
            o_ref[:, r * dim:(r + 1) * dim] = o_t[:, r * tq:(r + 1) * tq].T.astype(o_ref.dtype)
            pr = p[:, r * tq:(r + 1) * tq]
            psum = pr if psum is None else psum + pr
        imp_ref[...] = jnp.dot(selt_ref[:, :nk], psum, preferred_element_type=f32,
                               precision=lax.Precision.HIGHEST)

    n_vis = (q0 + tq - CMP_BLOCK) // CMP_STRIDE + 1
    n_chunks = ncp // chunk
    needed = jnp.minimum((n_vis + chunk - 1) // chunk, n_chunks)
    for k in range(1, n_chunks + 1):
        pl.when(jnp.maximum(needed, 1) == k)(functools.partial(attend, k * chunk))

    imp = imp_ref[...]
    j = lax.broadcasted_iota(jnp.int32, (nsp, tq), 0)
    tt = q0 + lax.broadcasted_iota(jnp.int32, (nsp, tq), 1)
    cur = tt // SLC_BLOCK
    forced = (j == 0) | (j == cur) | (j == cur - 1)
    valid = j * SLC_BLOCK <= tt
    score = jnp.where(forced, jnp.inf, jnp.where(valid, imp, -jnp.inf))
    chosen = jnp.zeros((nsp, tq), f32)
    jf = j.astype(f32)
    for _ in range(n_sel):
        best = jnp.max(score, axis=0, keepdims=True)
        first = jnp.min(jnp.where(score == best, jf, float(nsp)), axis=0, keepdims=True)
        hit = jf == first
        chosen = jnp.where(hit, 1.0, chosen)
        score = jnp.where(hit, -jnp.inf, score)
    mb_ref[...] = jnp.where(chosen > 0, 0.0, MASKED).T.astype(mb_ref.dtype)


def _cmp_attention(q_raw, kcmp, vcmp, sel_t, *, batch, seq, groups, rep, dim, tq, n_sel):
    nq = seq // tq
    ncp = kcmp.shape[2]
    nsp = sel_t.shape[0]
    chunk = min(LANES, ncp)
    kern = functools.partial(_cmp_kernel, rep=rep, dim=dim, tq=tq, n_sel=n_sel, chunk=chunk)
    kv_spec = pl.BlockSpec((None, None, ncp, dim), lambda b, g, i: (b, g, 0, 0))
    return pl.pallas_call(
        kern,
        out_shape=(jax.ShapeDtypeStruct((batch * seq, groups * rep * dim), bf16),
                   jax.ShapeDtypeStruct((batch, groups, seq, nsp), bf16)),
        grid=(batch, groups, nq),
        in_specs=[pl.BlockSpec((tq, rep * dim), lambda b, g, i: (b * nq + i, g)),
                  kv_spec, kv_spec,
                  pl.BlockSpec((nsp, ncp), lambda b, g, i: (0, 0))],
        out_specs=(pl.BlockSpec((tq, rep * dim), lambda b, g, i: (b * nq + i, g)),
                   pl.BlockSpec((None, None, tq, nsp), lambda b, g, i: (b, g, i, 0))),
        scratch_shapes=[pltpu.VMEM((nsp, tq), f32)],
        compiler_params=_params(("parallel", "parallel", "arbitrary")),
        name="cmp_attention",
    )(q_raw, kcmp, vcmp, sel_t)


def _slc_kernel(q_ref, mb_ref, k_ref, v_ref, o_ref, kp_ref, qp_ref, *, groups, rep, dim, tq, kt):
    i = pl.program_id(1)
    seq = k_ref.shape[0]
    nsp = mb_ref.shape[-1]
    rows = rep * tq

    @pl.when(i == 0)
    def _():
        blk = lax.broadcasted_iota(jnp.int32, (seq, nsp), 0) // SLC_BLOCK
        lane = lax.broadcasted_iota(jnp.int32, (seq, nsp), 1)
        onehot = jnp.where(blk == lane, 1.0, 0.0).astype(bf16)
        for g in range(groups):
            kp_ref[g, :, :dim] = k_ref[:, g * dim:(g + 1) * dim]
            kp_ref[g, :, dim:] = onehot

    for g in range(groups):
        mb = mb_ref[g]
        for r in range(rep):
            h = g * rep + r
            qp_ref[g, r * tq:(r + 1) * tq, :dim] = q_ref[:, h * dim:(h + 1) * dim]
            qp_ref[g, r * tq:(r + 1) * tq, dim:] = mb

    def update(g, carry, k0, bias):
        m, l, acc = carry
        s = lax.dot_general(kp_ref[g, pl.ds(k0, kt), :], qp_ref[g], _NT, preferred_element_type=f32)
        if bias is not None:
            s = s + bias
        m_new = jnp.maximum(m, jnp.max(s, axis=0, keepdims=True))
        alpha = jnp.exp(m - m_new)
        p = jnp.exp(s - m_new)
        l = alpha * l + jnp.sum(p, axis=0, keepdims=True)
        acc = alpha * acc + lax.dot_general(v_ref[pl.ds(k0, kt), g * dim:(g + 1) * dim], p.astype(bf16),
                                            _TN, preferred_element_type=f32)
        return m_new, l, acc

    def body(t, carries):
        k0 = pl.multiple_of(t * kt, kt)
        return tuple(update(g, carries[g], k0, None) for g in range(groups))

    n_full = (i * tq) // kt
    init = (jnp.full((1, rows), M_INIT, f32), jnp.zeros((1, rows), f32), jnp.zeros((dim, rows), f32))
    carries = lax.fori_loop(0, n_full, body, (init,) * groups)
    k0 = pl.multiple_of(n_full * kt, kt)
    kpos = k0 + lax.broadcasted_iota(jnp.int32, (kt, tq), 0)
    tpos = i * tq + lax.broadcasted_iota(jnp.int32, (kt, tq), 1)
    causal = _lane_tile(jnp.where(kpos <= tpos, 0.0, MASKED), rep)
    for g in range(groups):
        _, l, acc = update(g, carries[g], k0, causal)
        o_t = acc * (1.0 / l)
        for r in range(rep):
            h = g * rep + r
            o_ref[:, h * dim:(h + 1) * dim] = o_t[:, r * tq:(r + 1) * tq].T.astype(o_ref.dtype)


def _slc_attention(q_rot, mbias, k_rot, v, *, batch, seq, groups, rep, dim, tq, kt,
                   k_col, v_col):
    nq = seq // tq
    nsp = mbias.shape[-1]
    hw, gw = groups * rep * dim, groups * dim
    kern = functools.partial(_slc_kernel, groups=groups, rep=rep, dim=dim, tq=tq, kt=kt)
    return pl.pallas_call(
        kern,
        out_shape=jax.ShapeDtypeStruct((batch * seq, hw), bf16),
        grid=(batch, nq),
        in_specs=[pl.BlockSpec((tq, hw), lambda b, i: (b * nq + i, 0)),
                  pl.BlockSpec((None, groups, tq, nsp), lambda b, i: (b, 0, i, 0)),
                  pl.BlockSpec((seq, gw), lambda b, i: (b, k_col)),
                  pl.BlockSpec((seq, gw), lambda b, i: (b, v_col))],
        out_specs=pl.BlockSpec((tq, hw), lambda b, i: (b * nq + i, 0)),
        scratch_shapes=[pltpu.VMEM((groups, seq, dim + nsp), bf16),
                        pltpu.VMEM((groups, rep * tq, dim + nsp), bf16)],
        compiler_params=_params(("parallel", "arbitrary")),
        name="slc_attention",
    )(q_rot, mbias, k_rot, v)


def _merge_kernel(oa_ref, oc_ref, os_ref, ow_ref, gb_ref, ga_ref, gbm_ref, woa_ref, wob_ref,
                  o_ref, ob_ref, *, heads, dim):
    @pl.when(pl.program_id(1) == 0)
    def _():
        g = _sigmoid(gb_ref[...].astype(f32))
        for h in range(heads):
            cols = slice(h * dim, (h + 1) * dim)
            ob = (g[:, 3 * h:3 * h + 1] * oc_ref[:, cols].astype(f32)
                  + g[:, 3 * h + 1:3 * h + 2] * os_ref[:, cols].astype(f32)
                  + g[:, 3 * h + 2:3 * h + 3] * ow_ref[:, cols].astype(f32))
            ob_ref[:, cols] = ob.astype(bf16)

    ya = jnp.dot(oa_ref[...], woa_ref[...], preferred_element_type=f32)
    yb = jnp.dot(ob_ref[...], wob_ref[...], preferred_element_type=f32)
    merged = _sigmoid(ga_ref[...].astype(f32)) * ya + _sigmoid(gbm_ref[...].astype(f32)) * yb
    o_ref[...] = merged.astype(o_ref.dtype)


def _merge(o_a, o_cmp, o_slc, o_win, proj_c, w_oa, w_ob, *, tm, tn, gb_col, ga_col, gbm_col):
    n, da = o_a.shape
    db = o_cmp.shape[1]
    d = w_oa.shape[1]
    row = lambda w: pl.BlockSpec((tm, w), lambda i, j: (i, 0))
    kern = functools.partial(_merge_kernel, heads=B_HEADS, dim=B_HEAD_DIM)
    return pl.pallas_call(
        kern,
        out_shape=jax.ShapeDtypeStruct((n, d), bf16),
        grid=(n // tm, d // tn),
        in_specs=[row(da), row(db), row(db), row(db),
                  pl.BlockSpec((tm, LANES), lambda i, j: (i, gb_col)),
                  pl.BlockSpec((tm, tn), lambda i, j: (i, ga_col + j)),
                  pl.BlockSpec((tm, tn), lambda i, j: (i, gbm_col + j)),
                  pl.BlockSpec((da, tn), lambda i, j: (0, j)),
                  pl.BlockSpec((db, tn), lambda i, j: (0, j))],
        out_specs=pl.BlockSpec((tm, tn), lambda i, j: (i, j)),
        scratch_shapes=[pltpu.VMEM((tm, db), bf16)],
        compiler_params=_params(("parallel", "arbitrary")),
        name="merge",
    )(o_a, o_cmp, o_slc, o_win, proj_c, proj_c, proj_c, w_oa, w_ob)


def _matmul_res_kernel(x_ref, w_ref, r_ref, o_ref):
    o_ref[...] = r_ref[...] + jnp.dot(x_ref[...], w_ref[...], preferred_element_type=f32)


def _matmul_res(x, w, res, *, tm, tn):
    n, k = x.shape
    d = w.shape[1]
    return pl.pallas_call(
        _matmul_res_kernel,
        out_shape=jax.ShapeDtypeStruct((n, d), f32),
        grid=(n // tm, d // tn),
        in_specs=[pl.BlockSpec((tm, k), lambda i, j: (i, 0)),
                  pl.BlockSpec((k, tn), lambda i, j: (0, j)),
                  pl.BlockSpec((tm, tn), lambda i, j: (i, j))],
        out_specs=pl.BlockSpec((tm, tn), lambda i, j: (i, j)),
        compiler_params=_params(("parallel", "arbitrary")),
        name="matmul_res",
    )(x, w, res)


def _ple_kernel(xn_ref, h_ref, p_ref, wg_ref, wp_ref, fn_ref, o_ref, h4_ref):
    j = pl.program_id(1)
    nj = pl.num_programs(1)
    tn = h_ref.shape[1]
    gate = _sigmoid(jnp.dot(xn_ref[...], wg_ref[...], preferred_element_type=f32))
    emb = jnp.dot(p_ref[...].astype(bf16), wp_ref[...], preferred_element_type=f32)
    h4_ref[j] = h_ref[...] + emb * gate

    @pl.when(j == nj - 1)
    def _():
        n_tiles = h4_ref.shape[0]
        ss = None
        for c in range(n_tiles):
            t = h4_ref[c]
            part = jnp.sum(t * t, axis=-1, keepdims=True)
            ss = part if ss is None else ss + part
        inv = lax.rsqrt(ss / (n_tiles * tn) + EPS)
        for c in range(n_tiles):
            o_ref[:, c * tn:(c + 1) * tn] = h4_ref[c] * inv * fn_ref[:, c * tn:(c + 1) * tn]


def _ple(xn, h, p, w_gate, w_proj, final_norm, *, tm, tn):
    n, d = h.shape
    pd = p.shape[1]
    return pl.pallas_call(
        _ple_kernel,
        out_shape=jax.ShapeDtypeStruct((n, d), f32),
        grid=(n // tm, d // tn),
        in_specs=[pl.BlockSpec((tm, d), lambda i, j: (i, 0)),
                  pl.BlockSpec((tm, tn), lambda i, j: (i, j)),
                  pl.BlockSpec((tm, pd), lambda i, j: (i, 0)),
                  pl.BlockSpec((d, tn), lambda i, j: (0, j)),
                  pl.BlockSpec((pd, tn), lambda i, j: (0, j)),
                  pl.BlockSpec((1, d), lambda i, j: (0, 0))],
        out_specs=pl.BlockSpec((tm, d), lambda i, j: (i, 0)),
        scratch_shapes=[pltpu.VMEM((d // tn, tm, tn), f32)],
        compiler_params=_params(("parallel", "arbitrary")),
        name="ple_final",
    )(xn, h, p, w_gate, w_proj, final_norm.reshape(1, d))


def _selection_map_t(nc, ns, ncp, nsp):
    a, b = SLC_BLOCK // CMP_STRIDE, CMP_BLOCK // CMP_STRIDE
    j = np.arange(ns)[:, None, None]
    c = a * j - np.arange(a)[None, :, None] - np.arange(b)[None, None, :]
    jj = np.broadcast_to(j, c.shape)
    ok = (c >= 0) & (c < nc)
    mat = np.zeros((nsp, ncp), np.float32)
    np.add.at(mat, (jj[ok], c[ok]), 1.0)
    return jnp.asarray(mat)


def _split_w_in(w_in):
    qa_w, kv_a = A_HEADS * A_HEAD_DIM, A_KV_HEADS * A_HEAD_DIM
    qb_w, kv_b = B_HEADS * B_HEAD_DIM, B_KV_GROUPS * B_HEAD_DIM
    d_model = w_in.shape[0]
    sizes = [qa_w, kv_a, kv_a, qb_w] + [kv_b] * 6 + [B_HEADS * 3, d_model, d_model]
    offs = np.concatenate([[0], np.cumsum(sizes)])
    qa, ka, va, qb, kc, vc, ksl, vsl, kwn, vwn, gb, ga, gbm = [
        w_in[:, int(offs[i]):int(offs[i + 1])] for i in range(len(sizes))]
    gb_pad = jnp.zeros((d_model, 2 * LANES - B_HEADS * 3), w_in.dtype)
    w_a = jnp.concatenate([qa, ka], axis=1).astype(bf16)
    w_b = jnp.concatenate([qb, ksl, kwn], axis=1).astype(bf16)
    w_c = jnp.concatenate([ga, gbm, va, kc, vc, vsl, vwn, gb, gb_pad], axis=1).astype(bf16)
    return w_a, w_b, w_c


def _layer(h, tabs, w, *, batch, seq):
    n, d_model = h.shape
    c64, s64, c128, s128 = tabs
    kv_a = A_KV_HEADS * A_HEAD_DIM
    kv_b = B_KV_GROUPS * B_HEAD_DIM
    qb_w = B_HEADS * B_HEAD_DIM
    rep_b = B_HEADS // B_KV_GROUPS

    h = _ffn(h, w["ffn1_norm"], w["ffn1_w_gate"], w["ffn1_w_up"], w["ffn1_w_down"], tm=512, ff=w["d_ff"])

    xn = _norm_cast(h, w["mix_norm"], tm=512)
    w_a, w_b, w_c = w["w_in_split"]
    proj_a = _proj(xn, w_a, tm=1024, tn=768, mode="rope64", tables=(c64, s64))
    proj_b_rot, proj_b_raw = _proj(xn, w_b, tm=1024, tn=512, mode="rope128", tables=(c128, s128),
                                   n_scaled_tiles=qb_w // 512, scale=B_HEAD_DIM ** -0.5)
    c0 = 2 * d_model
    proj_c = _proj(xn, w_c, tm=1024, tn=512, mode="plain")

    o_a = _banded(proj_a, proj_a, proj_c, w["sinks"], batch=batch, seq=seq,
                  q_col=0, k_col=(A_HEADS * A_HEAD_DIM) // kv_a, v_col=c0 // kv_a,
                  groups=A_KV_HEADS, rep=A_HEADS // A_KV_HEADS, dim=A_HEAD_DIM,
                  window=A_WINDOW, use_sink=True, q_scale=A_HEAD_DIM ** -0.5)

    ncp = seq // CMP_STRIDE
    nc = (seq - CMP_BLOCK) // CMP_STRIDE + 1
    ns = seq // SLC_BLOCK
    kvc = proj_c[:, c0 + kv_a:c0 + kv_a + 2 * kv_b].reshape(batch, ncp, CMP_STRIDE * 2 * kv_b)
    kcmp, vcmp = _compress(kvc, w["nsa_pe_k"], w["nsa_w_ck1"], w["nsa_w_ck2"],
                           w["nsa_pe_v"], w["nsa_w_cv1"], w["nsa_w_cv2"],
                           batch=batch, groups=B_KV_GROUPS, dim=B_HEAD_DIM, n_real=nc)
    sel_t = _selection_map_t(nc, ns, ncp, LANES)
    o_cmp, mbias = _cmp_attention(proj_b_raw, kcmp, vcmp, sel_t, batch=batch, seq=seq,
                                  groups=B_KV_GROUPS, rep=rep_b, dim=B_HEAD_DIM, tq=128,
                                  n_sel=min(N_SELECT, ns))
    o_slc = _slc_attention(proj_b_rot, mbias, proj_b_rot, proj_c, batch=batch, seq=seq,
                           groups=B_KV_GROUPS, rep=rep_b, dim=B_HEAD_DIM, tq=256,
                           kt=min(1024, seq),
                           k_col=qb_w // kv_b, v_col=(c0 + kv_a + 2 * kv_b) // kv_b)
    o_win = _banded(proj_b_rot, proj_b_rot, proj_c, w["sinks"], batch=batch, seq=seq,
                    q_col=0, k_col=(qb_w + kv_b) // kv_b, v_col=(c0 + kv_a + 3 * kv_b) // kv_b,
                    groups=B_KV_GROUPS, rep=rep_b, dim=B_HEAD_DIM, window=B_WINDOW,
                    use_sink=False)

    tn = 1024
    merged = _merge(o_a, o_cmp, o_slc, o_win, proj_c, w["w_o_a"], w["w_o_b"], tm=512, tn=tn,
                    gb_col=(c0 + kv_a + 4 * kv_b) // LANES, ga_col=0, gbm_col=d_model // tn)
    h = _matmul_res(merged, w["w_o"], h, tm=512, tn=1024)

    h = _ffn(h, w["ffn2_norm"], w["ffn2_w_gate"], w["ffn2_w_up"], w["ffn2_w_down"], tm=512, ff=w["d_ff"])
    return h


def kernel(x, p, positions, ffn1_norm, ffn1_w_gate, ffn1_w_up, ffn1_w_down, mix_norm, w_in, sinks, nsa_pe_k, nsa_w_ck1, nsa_w_ck2, nsa_pe_v, nsa_w_cv1, nsa_w_cv2, w_o_a, w_o_b, w_o, ffn2_norm, ffn2_w_gate, ffn2_w_up, ffn2_w_down, ple_norm, w_ple_gate, w_ple_proj, final_norm):
    batch, seq, d_model = x.shape
    depth = ffn1_norm.shape[0]
    assert depth == 1, "the final norm is fused into the last layer's embedding step"
    assert seq // SLC_BLOCK <= LANES and seq % 512 == 0
    n = batch * seq
    h = x.reshape(n, d_model)
    tabs = _rope_tables(positions, tm=1024)
    i = 0
    small = lambda a: a[i].astype(bf16)
    rows = lambda a: _cast_bf16(a[i], bk=min(FFN_TILE, a.shape[1]), bn=a.shape[2])
    tiles = lambda a: _cast_bf16(a[i], bk=a.shape[1], bn=FFN_TILE, tiled=True)
    w = dict(
        d_ff=ffn1_w_gate.shape[2], ffn1_norm=ffn1_norm[i], ffn1_w_gate=tiles(ffn1_w_gate), ffn1_w_up=tiles(ffn1_w_up),
        ffn1_w_down=rows(ffn1_w_down), mix_norm=mix_norm[i], w_in_split=_split_w_in(w_in[i]),
        sinks=sinks[i], nsa_pe_k=nsa_pe_k[i], nsa_w_ck1=small(nsa_w_ck1), nsa_w_ck2=small(nsa_w_ck2),
        nsa_pe_v=nsa_pe_v[i], nsa_w_cv1=small(nsa_w_cv1), nsa_w_cv2=small(nsa_w_cv2),
        w_o_a=rows(w_o_a), w_o_b=rows(w_o_b), w_o=rows(w_o),
        ffn2_norm=ffn2_norm[i], ffn2_w_gate=tiles(ffn2_w_gate), ffn2_w_up=tiles(ffn2_w_up),
        ffn2_w_down=rows(ffn2_w_down))
    h = _layer(h, tabs, w, batch=batch, seq=seq)
    out = _ple(_norm_cast(h, ple_norm[i], tm=512), h, p[i].reshape(n, -1), rows(w_ple_gate),
               rows(w_ple_proj), final_norm, tm=512, tn=512)
    return out.reshape(batch, seq, d_model)
```

```python
import functools

import numpy as np
import jax
import jax.numpy as jnp
from jax import lax
from jax.experimental import pallas as pl
from jax.experimental.pallas import tpu as pltpu

f32 = jnp.float32
bf16 = jnp.bfloat16

A_HEADS, A_KV_HEADS, A_HEAD_DIM, A_WINDOW = 32, 4, 64, 128
B_HEADS, B_KV_GROUPS, B_HEAD_DIM = 16, 2, 128
CMP_BLOCK, CMP_STRIDE = 32, 16
SLC_BLOCK, N_SELECT, B_WINDOW = 64, 16, 512
BAND_BLOCK = 128
ROPE_THETA = 10000.0
EPS = 1e-6

LANES = 128
VMEM_LIMIT = 56 * 1024 * 1024
VMEM_LIMIT_BIG = 60 * 1024 * 1024
FFN_TILE = 512
FFN_DOWN_CHUNK = 1024
FFN_NORM_ROWS = 128
MASKED = -1e30
M_INIT = -1e29

_NT = (((1,), (1,)), ((), ()))
_TN = (((0,), (0,)), ((), ()))


def _params(sem, limit=VMEM_LIMIT):
    return pltpu.CompilerParams(dimension_semantics=sem, vmem_limit_bytes=limit)


def _rmsnorm(x, g):
    var = jnp.mean(x * x, axis=-1, keepdims=True)
    return x * lax.rsqrt(var + EPS) * g


def _sigmoid(x):
    return 1.0 / (1.0 + jnp.exp(-x))


def _lane_tile(x, n):
    return x if n == 1 else jnp.concatenate([x] * n, axis=1)


def _gate_row(gate_ref, branch, heads):
    g_t = _sigmoid(gate_ref[...].astype(f32)).T
    return jnp.concatenate([g_t[3 * h + branch:3 * h + branch + 1, :] for h in heads], axis=1)


def _cast_kernel(w_ref, o_ref, *, k, n):
    bk, bn = w_ref.shape
    w = w_ref[...]
    if k % bk or n % bn:
        row = pl.program_id(0) * bk + lax.broadcasted_iota(jnp.int32, (bk, bn), 0)
        col = pl.program_id(1) * bn + lax.broadcasted_iota(jnp.int32, (bk, bn), 1)
        w = jnp.where((row < k) & (col < n), w, 0.0)
    o_ref[...] = w.astype(o_ref.dtype)


def _cast_bf16(w, *, bk, bn, tiled=False):
    k, n = w.shape
    gk, gn = pl.cdiv(k, bk), pl.cdiv(n, bn)
    if tiled:
        out = jax.ShapeDtypeStruct((gn, gk * bk, bn), bf16)
        o_spec = pl.BlockSpec((None, bk, bn), lambda i, j: (j, i, 0))
    else:
        out = jax.ShapeDtypeStruct((gk * bk, gn * bn), bf16)
        o_spec = pl.BlockSpec((bk, bn), lambda i, j: (i, j))
    return pl.pallas_call(
        functools.partial(_cast_kernel, k=k, n=n),
        out_shape=out,
        grid=(gk, gn),
        in_specs=[pl.BlockSpec((bk, bn), lambda i, j: (i, j))],
        out_specs=o_spec,
        compiler_params=_params(("parallel", "parallel")),
        name="cast_bf16",
    )(w)


def _ffn_kernel(x_ref, g_ref, wg_ref, wu_ref, wd_ref, o_ref, xn_ref, *, tail):
    j = pl.program_id(1)

    @pl.when(j == 0)
    def _():
        for c in range(x_ref.shape[0] // FFN_NORM_ROWS):
            rows = slice(c * FFN_NORM_ROWS, (c + 1) * FFN_NORM_ROWS)
            x = x_ref[rows, :]
            xn_ref[rows, :] = _rmsnorm(x, g_ref[...]).astype(bf16)
            o_ref[rows, :] = x

    def step(width):
        xn = xn_ref[...]
        gate = jnp.dot(xn, wg_ref[:, :width], preferred_element_type=f32)
        up = jnp.dot(xn, wu_ref[:, :width], preferred_element_type=f32)
        act = (0.5 * gate * _sigmoid(gate) * up).astype(bf16)
        d = o_ref.shape[1]
        for c in range(d // FFN_DOWN_CHUNK):
            cols = slice(c * FFN_DOWN_CHUNK, (c + 1) * FFN_DOWN_CHUNK)
            o_ref[:, cols] += jnp.dot(act, wd_ref[:width, cols], preferred_element_type=f32)

    tf = wg_ref.shape[1]
    if tail == tf:
        step(tf)
    else:
        last = pl.num_programs(1) - 1
        pl.when(j != last)(lambda: step(tf))
        pl.when(j == last)(lambda: step(tail))


def _ffn(x, g, wg, wu, wd, *, tm, ff):
    n, d = x.shape
    n_tiles, _, tf = wg.shape
    once = dict(pipeline_mode=pl.Buffered(1))
    return pl.pallas_call(
        functools.partial(_ffn_kernel, tail=ff - (n_tiles - 1) * tf),
        out_shape=jax.ShapeDtypeStruct((n, d), f32),
        grid=(n // tm, n_tiles),
        in_specs=[
            pl.BlockSpec((tm, d), lambda i, j: (i, 0), **once),
            pl.BlockSpec((1, d), lambda i, j: (0, 0)),
            pl.BlockSpec((None, d, tf), lambda i, j: (j, 0, 0)),
            pl.BlockSpec((None, d, tf), lambda i, j: (j, 0, 0)),
            pl.BlockSpec((tf, d), lambda i, j: (j, 0)),
        ],
        out_specs=pl.BlockSpec((tm, d), lambda i, j: (i, 0)),
        scratch_shapes=[pltpu.VMEM((tm, d), bf16)],
        compiler_params=_params(("parallel", "arbitrary"), VMEM_LIMIT_BIG),
        name="ffn",
    )(x, g.reshape(1, d), wg, wu, wd)


def _norm_cast_kernel(x_ref, g_ref, o_ref):
    o_ref[...] = _rmsnorm(x_ref[...], g_ref[...]).astype(bf16)


def _norm_cast(x, g, *, tm):
    n, d = x.shape
    return pl.pallas_call(
        _norm_cast_kernel,
        out_shape=jax.ShapeDtypeStruct((n, d), bf16),
        grid=(n // tm,),
        in_specs=[pl.BlockSpec((tm, d), lambda i: (i, 0)),
                  pl.BlockSpec((1, d), lambda i: (0, 0))],
        out_specs=pl.BlockSpec((tm, d), lambda i: (i, 0)),
        compiler_params=_params(("parallel",)),
        name="norm_cast",
    )(x, g.reshape(1, d))


def _rope_table_kernel(pos_ref, inv64_ref, inv128_ref, c64_ref, s64_ref, c128_ref, s128_ref):
    pos = pos_ref[...].astype(f32)
    lane = lax.broadcasted_iota(jnp.int32, c64_ref.shape, 1)
    a64 = pos * inv64_ref[...]
    a128 = pos * inv128_ref[...]
    c64_ref[...] = jnp.cos(a64)
    s64_ref[...] = jnp.where(lane % A_HEAD_DIM < A_HEAD_DIM // 2, -1.0, 1.0) * jnp.sin(a64)
    c128_ref[...] = jnp.cos(a128)
    s128_ref[...] = jnp.where(lane < B_HEAD_DIM // 2, -1.0, 1.0) * jnp.sin(a128)


def _rope_tables(positions, *, tm):
    n = positions.size

    def inv(d):
        v = jnp.power(jnp.float32(ROPE_THETA), -jnp.arange(0, d, 2, dtype=f32) / d)
        return jnp.tile(v, LANES // (d // 2)).reshape(1, LANES)

    tab = jax.ShapeDtypeStruct((n, LANES), f32)
    row = pl.BlockSpec((1, LANES), lambda i: (0, 0))
    blk = pl.BlockSpec((tm, LANES), lambda i: (i, 0))
    return pl.pallas_call(
        _rope_table_kernel,
        out_shape=(tab, tab, tab, tab),
        grid=(n // tm,),
        in_specs=[pl.BlockSpec((tm, 1), lambda i: (i, 0)), row, row],
        out_specs=(blk, blk, blk, blk),
        compiler_params=_params(("parallel",)),
        name="rope_tables",
    )(positions.reshape(n, 1), inv(A_HEAD_DIM), inv(B_HEAD_DIM))


def _proj_plain_kernel(x_ref, w_ref, o_ref):
    o_ref[...] = jnp.dot(x_ref[...], w_ref[...], preferred_element_type=f32).astype(o_ref.dtype)


def _proj_rope64_kernel(x_ref, w_ref, c_ref, s_ref, o_ref):
    acc = jnp.dot(x_ref[...], w_ref[...], preferred_element_type=f32)
    tn = acc.shape[1]
    half = A_HEAD_DIM // 2
    first = lax.broadcasted_iota(jnp.int32, c_ref.shape, 1) % A_HEAD_DIM < half
    cos, sin = c_ref[...], s_ref[...]
    for c in range(tn // LANES):
        seg = acc[:, c * LANES:(c + 1) * LANES]
        rot = jnp.where(first, pltpu.roll(seg, LANES - half, 1), pltpu.roll(seg, half, 1))
        o_ref[:, c * LANES:(c + 1) * LANES] = (seg * cos + rot * sin).astype(o_ref.dtype)


def _proj_rope128_kernel(x_ref, w_ref, c_ref, s_ref, o_rot_ref, o_raw_ref, *, n_scaled_tiles, scale):
    acc = jnp.dot(x_ref[...], w_ref[...], preferred_element_type=f32)
    acc = acc * jnp.where(pl.program_id(1) < n_scaled_tiles, scale, 1.0).astype(f32)
    tn = acc.shape[1]
    cos, sin = c_ref[...], s_ref[...]
    o_raw_ref[...] = acc.astype(o_raw_ref.dtype)
    for c in range(tn // LANES):
        seg = acc[:, c * LANES:(c + 1) * LANES]
        rot = pltpu.roll(seg, B_HEAD_DIM // 2, 1)
        o_rot_ref[:, c * LANES:(c + 1) * LANES] = (seg * cos + rot * sin).astype(o_rot_ref.dtype)


def _proj(xn, w, *, tm, tn, mode, tables=None, n_scaled_tiles=0, scale=1.0):
    n, d = xn.shape
    width = w.shape[1]
    x_spec = pl.BlockSpec((tm, d), lambda i, j: (i, 0))
    w_spec = pl.BlockSpec((d, tn), lambda i, j: (0, j))
    o_spec = pl.BlockSpec((tm, tn), lambda i, j: (i, j))
    t_spec = pl.BlockSpec((tm, LANES), lambda i, j: (i, 0))
    out = jax.ShapeDtypeStruct((n, width), bf16)
    common = dict(grid=(n // tm, width // tn),
                  compiler_params=_params(("parallel", "arbitrary")))
    if mode == "plain":
        return pl.pallas_call(_proj_plain_kernel, out_shape=out, in_specs=[x_spec, w_spec],
                              out_specs=o_spec, name="proj_plain", **common)(xn, w)
    if mode == "rope64":
        return pl.pallas_call(_proj_rope64_kernel, out_shape=out,
                              in_specs=[x_spec, w_spec, t_spec, t_spec],
                              out_specs=o_spec, name="proj_rope64", **common)(xn, w, *tables)
    kern = functools.partial(_proj_rope128_kernel, n_scaled_tiles=n_scaled_tiles, scale=scale)
    return pl.pallas_call(kern, out_shape=(out, out),
                          in_specs=[x_spec, w_spec, t_spec, t_spec],
                          out_specs=(o_spec, o_spec), name="proj_rope128", **common)(xn, w, *tables)


def _banded_kernel(sink_ref, q_ref, k_ref, v_ref, *rest, groups, rep, dim, window, prev,
                   tq, use_sink, q_scale, gate_branch):
    gate_ref, o_ref = rest if gate_branch is not None else (None,) + rest
    q0 = pl.program_id(1) * tq
    slab = prev + tq
    pack = LANES // dim
    cols = rep // pack
    start = pl.multiple_of(jnp.maximum(q0 - prev, 0), tq)
    kpos = start + lax.broadcasted_iota(jnp.int32, (slab, tq), 0)
    tpos = q0 + lax.broadcasted_iota(jnp.int32, (slab, tq), 1)
    diff = tpos - kpos
    bias = _lane_tile(jnp.where((diff >= 0) & (diff < window), 0.0, -jnp.inf), cols)
    zeros = jnp.zeros((slab, LANES - dim), bf16)
    for g in range(groups):
        q = jnp.concatenate([q_ref[:, (g * cols + c) * LANES:(g * cols + c + 1) * LANES]
                             for c in range(cols)], axis=0)
        if q_scale != 1.0:
            q = q * q_scale
        kg = k_ref[pl.ds(start, slab), g * dim:(g + 1) * dim]
        vg = v_ref[pl.ds(start, slab), g * dim:(g + 1) * dim]
        out_t = None
        for sub in range(pack):
            if pack == 1:
                kk, vv = kg, vg
            else:
                parts = lambda a: [a, zeros] if sub == 0 else [zeros, a]
                kk, vv = jnp.concatenate(parts(kg), axis=1), jnp.concatenate(parts(vg), axis=1)
            s = lax.dot_general(kk, q, _NT, preferred_element_type=f32) + bias
            m = jnp.max(s, axis=0, keepdims=True)
            if use_sink:
                sk = jnp.concatenate([jnp.full((1, tq), sink_ref[g * rep + pack * c + sub], f32)
                                      for c in range(cols)], axis=1)
                m = jnp.maximum(m, sk)
            e = jnp.exp(s - m)
            den = jnp.sum(e, axis=0, keepdims=True)
            if use_sink:
                den = den + jnp.exp(sk - m)
            o_t = lax.dot_general(vv, e.astype(bf16), _TN, preferred_element_type=f32)
            scale = 1.0 / den
            if gate_branch is not None:
                scale = scale * _gate_row(gate_ref, gate_branch,
                                          [g * rep + pack * c + sub for c in range(cols)])
            o_t = o_t * scale
            out_t = o_t if out_t is None else out_t + o_t
        for c in range(cols):
            o_ref[:, (g * cols + c) * LANES:(g * cols + c + 1) * LANES] = (
                out_t[:, c * tq:(c + 1) * tq].T.astype(o_ref.dtype))


def _banded(q, k, v, sinks, *, batch, seq, q_col, k_col, v_col, groups, rep, dim, window,
            use_sink, q_scale=1.0, gates=None, gate_col=0, gate_branch=None):
    tq = BAND_BLOCK
    prev = -(-(window - 1) // tq) * tq
    nq = seq // tq
    hw, gw = groups * rep * dim, groups * dim
    assert LANES % dim == 0 and rep % (LANES // dim) == 0
    kern = functools.partial(_banded_kernel, groups=groups, rep=rep, dim=dim, window=window,
                             prev=prev, tq=tq, use_sink=use_sink, q_scale=q_scale,
                             gate_branch=gate_branch)
    in_specs = [
        pl.BlockSpec(memory_space=pltpu.SMEM),
        pl.BlockSpec((tq, hw), lambda b, i: (b * nq + i, q_col)),
        pl.BlockSpec((seq, gw), lambda b, i: (b, k_col)),
        pl.BlockSpec((seq, gw), lambda b, i: (b, v_col)),
    ]
    args = [sinks, q, k, v]
    if gate_branch is not None:
        in_specs.append(pl.BlockSpec((tq, LANES), lambda b, i: (b * nq + i, gate_col)))
        args.append(gates)
    return pl.pallas_call(
        kern,
        out_shape=jax.ShapeDtypeStruct((batch * seq, hw), bf16),
        grid=(batch, nq),
        in_specs=in_specs,
        out_specs=pl.BlockSpec((tq, hw), lambda b, i: (b * nq + i, 0)),
        compiler_params=_params(("parallel", "arbitrary")),
        name="banded_sink" if use_sink else "banded",
    )(*args)


def _gelu_tanh(x):
    return x * (0.5 * (1.0 + jnp.tanh(np.sqrt(2.0 / np.pi).astype(np.float32)
                                      * (x + 0.044715 * (x * x * x)))))


def _compress_kernel(r_ref, pek_ref, wk1_ref, wk2_ref, pev_ref, wv1_ref, wv2_ref,
                     kc_ref, vc_ref, *, groups, dim, n_real):
    ncp = r_ref.shape[0]
    tok_w = 2 * groups * dim
    halves = CMP_BLOCK // CMP_STRIDE
    row = lax.broadcasted_iota(jnp.int32, (ncp, dim), 0)
    for which, (pe_ref, w1_ref, w2_ref, out_ref) in enumerate(
            ((pek_ref, wk1_ref, wk2_ref, kc_ref), (pev_ref, wv1_ref, wv2_ref, vc_ref))):
        for g in range(groups):
            h1 = None
            for half in range(halves):
                pieces = []
                for tl in range(CMP_STRIDE):
                    l = half * CMP_STRIDE + tl
                    col = tl * tok_w + which * groups * dim + g * dim
                    pieces.append((r_ref[:, col:col + dim].astype(f32) + pe_ref[l:l + 1, :]).astype(bf16))
                x = jnp.concatenate(pieces, axis=1)
                w = w1_ref[half * CMP_STRIDE * dim:(half + 1) * CMP_STRIDE * dim, :]
                part = jnp.dot(x, w, preferred_element_type=f32)
                if half:
                    part = pltpu.roll(part, ncp - half, 0)
                h1 = part if h1 is None else h1 + part
            y = _gelu_tanh(h1).astype(bf16)
            out = jnp.dot(y, w2_ref[...], preferred_element_type=f32)
            out_ref[g] = jnp.where(row < n_real, out, 0.0).astype(out_ref.dtype)


def _compress(r, pe_k, wk1, wk2, pe_v, wv1, wv2, *, batch, groups, dim, n_real):
    _, ncp, width = r.shape
    hidden = wk1.shape[1]
    full = lambda shape: pl.BlockSpec(shape, lambda b: (0,) * len(shape))
    out = jax.ShapeDtypeStruct((batch, groups, ncp, dim), bf16)
    o_spec = pl.BlockSpec((None, groups, ncp, dim), lambda b: (b, 0, 0, 0))
    kern = functools.partial(_compress_kernel, groups=groups, dim=dim, n_real=n_real)
    return pl.pallas_call(
        kern,
        out_shape=(out, out),
        grid=(batch,),
        in_specs=[pl.BlockSpec((None, ncp, width), lambda b: (b, 0, 0)),
                  full((CMP_BLOCK, dim)), full((CMP_BLOCK * dim, hidden)), full((hidden, dim)),
                  full((CMP_BLOCK, dim)), full((CMP_BLOCK * dim, hidden)), full((hidden, dim))],
        out_specs=(o_spec, o_spec),
        compiler_params=_params(("parallel",)),
        name="compress",
    )(r, pe_k, wk1, wk2, pe_v, wv1, wv2)


def _cmp_kernel(q_ref, kc_ref, vc_ref, selt_ref, gate_ref, o_ref, mb_ref, imp_ref, *, groups, rep, dim, tq,
                n_sel, chunk):
    q0 = pl.program_id(2) * tq
    ncp = kc_ref.shape[0]
    nsp = selt_ref.shape[0]
    q = jnp.concatenate([q_ref[:, r * dim:(r + 1) * dim] for r in range(rep)], axis=0)

    gate = _gate_row(gate_ref, 0, range(rep))
    for g in range(1, groups):
        gate = jnp.where(pl.program_id(1) == g, _gate_row(gate_ref, 0, range(g * rep, (g + 1) * rep)), gate)

    def attend(nk):
        cidx = lax.broadcasted_iota(jnp.int32, (nk, tq), 0)
        tpos = q0 + lax.broadcasted_iota(jnp.int32, (nk, tq), 1)
        bias = _lane_tile(jnp.where(cidx * CMP_STRIDE + (CMP_BLOCK - 1) <= tpos, 0.0, -jnp.inf), rep)
        s = lax.dot_general(kc_ref[:nk, :], q, _NT, preferred_element_type=f32) + bias
        m = jnp.max(s, axis=0, keepdims=True)
        m = jnp.where(m > -jnp.inf, m, 0.0)
        e = jnp.exp(s - m)
        den = jnp.sum(e, axis=0, keepdims=True)
        p = e * (1.0 / jnp.where(den > 0, den, 1.0))
        o_t = lax.dot_general(vc_ref[:nk, :], p.astype(bf16), _TN, preferred_element_type=f32)
        o_t = o_t * gate
        psum = None
        for r in range(rep):
            o_ref[:, r * dim:(r + 1) * dim] = o_t[:, r * tq:(r + 1) * tq].T.astype(o_ref.dtype)
            pr = p[:, r * tq:(r + 1) * tq]
            psum = pr if psum is None else psum + pr
        imp_ref[...] = jnp.dot(selt_ref[:, :nk], psum, preferred_element_type=f32,
                               precision=lax.Precision.HIGHEST)

    n_vis = (q0 + tq - CMP_BLOCK) // CMP_STRIDE + 1
    n_chunks = ncp // chunk
    needed = jnp.minimum((n_vis + chunk - 1) // chunk, n_chunks)
    for k in range(1, n_chunks + 1):
        pl.when(jnp.maximum(needed, 1) == k)(functools.partial(attend, k * chunk))

    imp = imp_ref[...]
    j = lax.broadcasted_iota(jnp.int32, (nsp, tq), 0)
    tt = q0 + lax.broadcasted_iota(jnp.int32, (nsp, tq), 1)
    cur = tt // SLC_BLOCK
    forced = (j == 0) | (j == cur) | (j == cur - 1)
    valid = j * SLC_BLOCK <= tt
    score = jnp.where(forced, jnp.inf, jnp.where(valid, imp, -jnp.inf))
    chosen = jnp.zeros((nsp, tq), f32)
    jf = j.astype(f32)
    for _ in range(n_sel):
        best = jnp.max(score, axis=0, keepdims=True)
        first = jnp.min(jnp.where(score == best, jf, float(nsp)), axis=0, keepdims=True)
        hit = jf == first
        chosen = jnp.where(hit, 1.0, chosen)
        score = jnp.where(hit, -jnp.inf, score)
    mb_ref[...] = jnp.where(chosen > 0, 0.0, MASKED).T.astype(mb_ref.dtype)


def _cmp_attention(q_raw, kcmp, vcmp, sel_t, gates, *, batch, seq, groups, rep, dim, tq, n_sel, gate_col):
    nq = seq // tq
    ncp = kcmp.shape[2]
    nsp = sel_t.shape[0]
    chunk = min(LANES, ncp)
    kern = functools.partial(_cmp_kernel, groups=groups, rep=rep, dim=dim, tq=tq, n_sel=n_sel, chunk=chunk)
    kv_spec = pl.BlockSpec((None, None, ncp, dim), lambda b, g, i: (b, g, 0, 0))
    return pl.pallas_call(
        kern,
        out_shape=(jax.ShapeDtypeStruct((batch * seq, groups * rep * dim), bf16),
                   jax.ShapeDtypeStruct((batch, groups, seq, nsp), bf16)),
        grid=(batch, groups, nq),
        in_specs=[pl.BlockSpec((tq, rep * dim), lambda b, g, i: (b * nq + i, g)),
                  kv_spec, kv_spec,
                  pl.BlockSpec((nsp, ncp), lambda b, g, i: (0, 0)),
                  pl.BlockSpec((tq, LANES), lambda b, g, i: (b * nq + i, gate_col))],
        out_specs=(pl.BlockSpec((tq, rep * dim), lambda b, g, i: (b * nq + i, g)),
                   pl.BlockSpec((None, None, tq, nsp), lambda b, g, i: (b, g, i, 0))),
        scratch_shapes=[pltpu.VMEM((nsp, tq), f32)],
        compiler_params=_params(("parallel", "parallel", "arbitrary")),
        name="cmp_attention",
    )(q_raw, kcmp, vcmp, sel_t, gates)


def _slc_kernel(q_ref, mb_ref, k_ref, v_ref, gate_ref, o_ref, kp_ref, qp_ref, *, groups, rep, dim, tq, kt):
    i = pl.program_id(1)
    seq = k_ref.shape[0]
    nsp = mb_ref.shape[-1]
    rows = rep * tq

    @pl.when(i == 0)
    def _():
        blk = lax.broadcasted_iota(jnp.int32, (seq, nsp), 0) // SLC_BLOCK
        lane = lax.broadcasted_iota(jnp.int32, (seq, nsp), 1)
        onehot = jnp.where(blk == lane, 1.0, 0.0).astype(bf16)
        for g in range(groups):
            kp_ref[g, :, :dim] = k_ref[:, g * dim:(g + 1) * dim]
            kp_ref[g, :, dim:] = onehot

    for g in range(groups):
        mb = mb_ref[g]
        for r in range(rep):
            h = g * rep + r
            qp_ref[g, r * tq:(r + 1) * tq, :dim] = q_ref[:, h * dim:(h + 1) * dim]
            qp_ref[g, r * tq:(r + 1) * tq, dim:] = mb

    def update(g, carry, k0, bias):
        m, l, acc = carry
        s = lax.dot_general(kp_ref[g, pl.ds(k0, kt), :], qp_ref[g], _NT, preferred_element_type=f32)
        if bias is not None:
            s = s + bias
        m_new = jnp.maximum(m, jnp.max(s, axis=0, keepdims=True))
        alpha = jnp.exp(m - m_new)
        p = jnp.exp(s - m_new)
        l = alpha * l + jnp.sum(p, axis=0, keepdims=True)
        acc = alpha * acc + lax.dot_general(v_ref[pl.ds(k0, kt), g * dim:(g + 1) * dim], p.astype(bf16),
                                            _TN, preferred_element_type=f32)
        return m_new, l, acc

    def body(t, carries):
        k0 = pl.multiple_of(t * kt, kt)
        return tuple(update(g, carries[g], k0, None) for g in range(groups))

    n_full = (i * tq) // kt
    init = (jnp.full((1, rows), M_INIT, f32), jnp.zeros((1, rows), f32), jnp.zeros((dim, rows), f32))
    carries = lax.fori_loop(0, n_full, body, (init,) * groups)
    k0 = pl.multiple_of(n_full * kt, kt)
    kpos = k0 + lax.broadcasted_iota(jnp.int32, (kt, tq), 0)
    tpos = i * tq + lax.broadcasted_iota(jnp.int32, (kt, tq), 1)
    causal = _lane_tile(jnp.where(kpos <= tpos, 0.0, MASKED), rep)
    for g in range(groups):
        _, l, acc = update(g, carries[g], k0, causal)
        o_t = acc * (_gate_row(gate_ref, 1, range(g * rep, (g + 1) * rep)) / l)
        for r in range(rep):
            h = g * rep + r
            o_ref[:, h * dim:(h + 1) * dim] = o_t[:, r * tq:(r + 1) * tq].T.astype(o_ref.dtype)


def _slc_attention(q_rot, mbias, k_rot, v, gates, *, batch, seq, groups, rep, dim, tq, kt,
                   k_col, v_col, gate_col):
    nq = seq // tq
    nsp = mbias.shape[-1]
    hw, gw = groups * rep * dim, groups * dim
    kern = functools.partial(_slc_kernel, groups=groups, rep=rep, dim=dim, tq=tq, kt=kt)
    return pl.pallas_call(
        kern,
        out_shape=jax.ShapeDtypeStruct((batch * seq, hw), bf16),
        grid=(batch, nq),
        in_specs=[pl.BlockSpec((tq, hw), lambda b, i: (b * nq + i, 0)),
                  pl.BlockSpec((None, groups, tq, nsp), lambda b, i: (b, 0, i, 0)),
                  pl.BlockSpec((seq, gw), lambda b, i: (b, k_col)),
                  pl.BlockSpec((seq, gw), lambda b, i: (b, v_col)),
                  pl.BlockSpec((tq, LANES), lambda b, i: (b * nq + i, gate_col))],
        out_specs=pl.BlockSpec((tq, hw), lambda b, i: (b * nq + i, 0)),
        scratch_shapes=[pltpu.VMEM((groups, seq, dim + nsp), bf16),
                        pltpu.VMEM((groups, rep * tq, dim + nsp), bf16)],
        compiler_params=_params(("parallel", "arbitrary")),
        name="slc_attention",
    )(q_rot, mbias, k_rot, v, gates)


def _merge_kernel(oa_ref, oc_ref, os_ref, ow_ref, ga_ref, gbm_ref, woa_ref, wob_ref, o_ref, ob_ref):
    @pl.when(pl.program_id(1) == 0)
    def _():
        ob = oc_ref[...].astype(f32) + os_ref[...].astype(f32) + ow_ref[...].astype(f32)
        ob_ref[...] = ob.astype(bf16)

    ya = jnp.dot(oa_ref[...], woa_ref[...], preferred_element_type=f32)
    yb = jnp.dot(ob_ref[...], wob_ref[...], preferred_element_type=f32)
    merged = _sigmoid(ga_ref[...].astype(f32)) * ya + _sigmoid(gbm_ref[...].astype(f32)) * yb
    o_ref[...] = merged.astype(o_ref.dtype)


def _merge(o_a, o_cmp, o_slc, o_win, proj_c, w_oa, w_ob, *, tm, tn, ga_col, gbm_col):
    n, da = o_a.shape
    db = o_cmp.shape[1]
    d = w_oa.shape[1]
    row = lambda w: pl.BlockSpec((tm, w), lambda i, j: (i, 0))
    return pl.pallas_call(
        _merge_kernel,
        out_shape=jax.ShapeDtypeStruct((n, d), bf16),
        grid=(n // tm, d // tn),
        in_specs=[row(da), row(db), row(db), row(db),
                  pl.BlockSpec((tm, tn), lambda i, j: (i, ga_col + j)),
                  pl.BlockSpec((tm, tn), lambda i, j: (i, gbm_col + j)),
                  pl.BlockSpec((da, tn), lambda i, j: (0, j)),
                  pl.BlockSpec((db, tn), lambda i, j: (0, j))],
        out_specs=pl.BlockSpec((tm, tn), lambda i, j: (i, j)),
        scratch_shapes=[pltpu.VMEM((tm, db), bf16)],
        compiler_params=_params(("parallel", "arbitrary")),
        name="merge",
    )(o_a, o_cmp, o_slc, o_win, proj_c, proj_c, w_oa, w_ob)


def _matmul_res_kernel(x_ref, w_ref, r_ref, o_ref):
    o_ref[...] = r_ref[...] + jnp.dot(x_ref[...], w_ref[...], preferred_element_type=f32)


def _matmul_res(x, w, res, *, tm, tn):
    n, k = x.shape
    d = w.shape[1]
    return pl.pallas_call(
        _matmul_res_kernel,
        out_shape=jax.ShapeDtypeStruct((n, d), f32),
        grid=(n // tm, d // tn),
        in_specs=[pl.BlockSpec((tm, k), lambda i, j: (i, 0)),
                  pl.BlockSpec((k, tn), lambda i, j: (0, j)),
                  pl.BlockSpec((tm, tn), lambda i, j: (i, j))],
        out_specs=pl.BlockSpec((tm, tn), lambda i, j: (i, j)),
        compiler_params=_params(("parallel", "arbitrary")),
        name="matmul_res",
    )(x, w, res)


def _ple_kernel(xn_ref, h_ref, p_ref, wg_ref, wp_ref, fn_ref, o_ref, h4_ref):
    j = pl.program_id(1)
    nj = pl.num_programs(1)
    tn = h_ref.shape[1]
    gate = _sigmoid(jnp.dot(xn_ref[...], wg_ref[...], preferred_element_type=f32))
    emb = jnp.dot(p_ref[...].astype(bf16), wp_ref[...], preferred_element_type=f32)
    h4_ref[j] = h_ref[...] + emb * gate

    @pl.when(j == nj - 1)
    def _():
        n_tiles = h4_ref.shape[0]
        ss = None
        for c in range(n_tiles):
            t = h4_ref[c]
            part = jnp.sum(t * t, axis=-1, keepdims=True)
            ss = part if ss is None else ss + part
        inv = lax.rsqrt(ss / (n_tiles * tn) + EPS)
        for c in range(n_tiles):
            o_ref[:, c * tn:(c + 1) * tn] = h4_ref[c] * inv * fn_ref[:, c * tn:(c + 1) * tn]


def _ple(xn, h, p, w_gate, w_proj, final_norm, *, tm, tn):
    n, d = h.shape
    pd = p.shape[1]
    return pl.pallas_call(
        _ple_kernel,
        out_shape=jax.ShapeDtypeStruct((n, d), f32),
        grid=(n // tm, d // tn),
        in_specs=[pl.BlockSpec((tm, d), lambda i, j: (i, 0), pipeline_mode=pl.Buffered(1)),
                  pl.BlockSpec((tm, tn), lambda i, j: (i, j)),
                  pl.BlockSpec((tm, pd), lambda i, j: (i, 0)),
                  pl.BlockSpec((d, tn), lambda i, j: (0, j)),
                  pl.BlockSpec((pd, tn), lambda i, j: (0, j)),
                  pl.BlockSpec((1, d), lambda i, j: (0, 0))],
        out_specs=pl.BlockSpec((tm, d), lambda i, j: (i, 0)),
        scratch_shapes=[pltpu.VMEM((d // tn, tm, tn), f32)],
        compiler_params=_params(("parallel", "arbitrary"), VMEM_LIMIT_BIG),
        name="ple_final",
    )(xn, h, p, w_gate, w_proj, final_norm.reshape(1, d))


def _selection_map_t(nc, ns, ncp, nsp):
    a, b = SLC_BLOCK // CMP_STRIDE, CMP_BLOCK // CMP_STRIDE
    j = np.arange(ns)[:, None, None]
    c = a * j - np.arange(a)[None, :, None] - np.arange(b)[None, None, :]
    jj = np.broadcast_to(j, c.shape)
    ok = (c >= 0) & (c < nc)
    mat = np.zeros((nsp, ncp), np.float32)
    np.add.at(mat, (jj[ok], c[ok]), 1.0)
    return jnp.asarray(mat)


def _regroup_kernel(ia_ref, ib_ref, a_ref, b_ref, o_ref, *, n_shifted, shift):
    del ia_ref, ib_ref
    if n_shifted:
        @pl.when(pl.program_id(1) < n_shifted)
        def _():
            o_ref[...] = jnp.concatenate([a_ref[:, shift:], b_ref[:, :shift]], axis=1).astype(o_ref.dtype)

        @pl.when(pl.program_id(1) >= n_shifted)
        def _():
            o_ref[...] = a_ref[...].astype(o_ref.dtype)
    else:
        o_ref[...] = a_ref[...].astype(o_ref.dtype)


def _regroup_cast(w, blocks_a, blocks_b, *, n_shifted=0, shift=0, bk, bn):
    k = w.shape[0]
    nb = len(blocks_a)
    kern = functools.partial(_regroup_kernel, n_shifted=n_shifted, shift=shift)
    return pl.pallas_call(
        kern,
        out_shape=jax.ShapeDtypeStruct((k, nb * bn), bf16),
        grid_spec=pltpu.PrefetchScalarGridSpec(
            num_scalar_prefetch=2,
            grid=(k // bk, nb),
            in_specs=[pl.BlockSpec((bk, bn), lambda i, j, ia, ib: (i, ia[j])),
                      pl.BlockSpec((bk, bn), lambda i, j, ia, ib: (i, ib[j]))],
            out_specs=pl.BlockSpec((bk, bn), lambda i, j, ia, ib: (i, j))),
        compiler_params=_params(("parallel", "arbitrary")),
        name="regroup_cast",
    )(jnp.asarray(blocks_a, jnp.int32), jnp.asarray(blocks_b, jnp.int32), w, w)


def _split_w_in(w_in):
    bn = A_KV_HEADS * A_HEAD_DIM
    assert bn == B_KV_GROUPS * B_HEAD_DIM
    d_model = w_in.shape[0]
    qa_b, qb_b, gate_b = A_HEADS * A_HEAD_DIM // bn, B_HEADS * B_HEAD_DIM // bn, d_model // bn
    qa0 = 0
    ka0 = qa0 + qa_b
    va0 = ka0 + 1
    qb0 = va0 + 1
    kc0 = qb0 + qb_b
    vc0, ksl0, vsl0, kwn0, vwn0, gb0 = (kc0 + t for t in range(1, 7))
    gate_shift = B_HEADS * 3
    blocks_a = list(range(qa0, ka0 + 1))
    blocks_b = list(range(qb0, kc0)) + [ksl0, kwn0]
    gates = list(range(gb0, gb0 + 2 * gate_b))
    blocks_c = gates + [va0, kc0, vc0, vsl0, vwn0, gb0]
    blocks_c_next = [g + 1 for g in gates] + [va0, kc0, vc0, vsl0, vwn0, gb0]
    kw = dict(bk=d_model // 2, bn=bn)
    w_a = _regroup_cast(w_in, blocks_a, blocks_a, **kw)
    w_b = _regroup_cast(w_in, blocks_b, blocks_b, **kw)
    w_c = _regroup_cast(w_in, blocks_c, blocks_c_next, n_shifted=len(gates), shift=gate_shift, **kw)
    return w_a, w_b, w_c


def _layer(h, tabs, w, *, batch, seq):
    n, d_model = h.shape
    c64, s64, c128, s128 = tabs
    kv_a = A_KV_HEADS * A_HEAD_DIM
    kv_b = B_KV_GROUPS * B_HEAD_DIM
    qb_w = B_HEADS * B_HEAD_DIM
    rep_b = B_HEADS // B_KV_GROUPS

    h = _ffn(h, w["ffn1_norm"], w["ffn1_w_gate"], w["ffn1_w_up"], w["ffn1_w_down"], tm=512, ff=w["d_ff"])

    xn = _norm_cast(h, w["mix_norm"], tm=512)
    w_a, w_b, w_c = w["w_in_split"]
    proj_a = _proj(xn, w_a, tm=1024, tn=768, mode="rope64", tables=(c64, s64))
    proj_b_rot, proj_b_raw = _proj(xn, w_b, tm=1024, tn=512, mode="rope128", tables=(c128, s128),
                                   n_scaled_tiles=qb_w // 512, scale=B_HEAD_DIM ** -0.5)
    c0 = 2 * d_model
    proj_c = _proj(xn, w_c, tm=1024, tn=512, mode="plain")

    o_a = _banded(proj_a, proj_a, proj_c, w["sinks"], batch=batch, seq=seq,
                  q_col=0, k_col=(A_HEADS * A_HEAD_DIM) // kv_a, v_col=c0 // kv_a,
                  groups=A_KV_HEADS, rep=A_HEADS // A_KV_HEADS, dim=A_HEAD_DIM,
                  window=A_WINDOW, use_sink=True, q_scale=A_HEAD_DIM ** -0.5)

    ncp = seq // CMP_STRIDE
    nc = (seq - CMP_BLOCK) // CMP_STRIDE + 1
    ns = seq // SLC_BLOCK
    kvc = proj_c[:, c0 + kv_a:c0 + kv_a + 2 * kv_b].reshape(batch, ncp, CMP_STRIDE * 2 * kv_b)
    kcmp, vcmp = _compress(kvc, w["nsa_pe_k"], w["nsa_w_ck1"], w["nsa_w_ck2"],
                           w["nsa_pe_v"], w["nsa_w_cv1"], w["nsa_w_cv2"],
                           batch=batch, groups=B_KV_GROUPS, dim=B_HEAD_DIM, n_real=nc)
    sel_t = _selection_map_t(nc, ns, ncp, LANES)
    gate_col = (c0 + kv_a + 4 * kv_b) // LANES
    o_cmp, mbias = _cmp_attention(proj_b_raw, kcmp, vcmp, sel_t, proj_c, batch=batch, seq=seq,
                                  groups=B_KV_GROUPS, rep=rep_b, dim=B_HEAD_DIM, tq=128,
                                  n_sel=min(N_SELECT, ns), gate_col=gate_col)
    o_slc = _slc_attention(proj_b_rot, mbias, proj_b_rot, proj_c, proj_c, batch=batch, seq=seq,
                           groups=B_KV_GROUPS, rep=rep_b, dim=B_HEAD_DIM, tq=256,
                           kt=min(1024, seq),
                           k_col=qb_w // kv_b, v_col=(c0 + kv_a + 2 * kv_b) // kv_b, gate_col=gate_col)
    o_win = _banded(proj_b_rot, proj_b_rot, proj_c, w["sinks"], batch=batch, seq=seq,
                    q_col=0, k_col=(qb_w + kv_b) // kv_b, v_col=(c0 + kv_a + 3 * kv_b) // kv_b,
                    groups=B_KV_GROUPS, rep=rep_b, dim=B_HEAD_DIM, window=B_WINDOW,
                    use_sink=False, gates=proj_c, gate_col=gate_col, gate_branch=2)

    tn = 1024
    merged = _merge(o_a, o_cmp, o_slc, o_win, proj_c, w["w_o_a"], w["w_o_b"], tm=512, tn=tn,
                    ga_col=0, gbm_col=d_model // tn)
    h = _matmul_res(merged, w["w_o"], h, tm=1024, tn=512)

    h = _ffn(h, w["ffn2_norm"], w["ffn2_w_gate"], w["ffn2_w_up"], w["ffn2_w_down"], tm=512, ff=w["d_ff"])
    return h


def kernel(x, p, positions, ffn1_norm, ffn1_w_gate, ffn1_w_up, ffn1_w_down, mix_norm, w_in, sinks, nsa_pe_k, nsa_w_ck1, nsa_w_ck2, nsa_pe_v, nsa_w_cv1, nsa_w_cv2, w_o_a, w_o_b, w_o, ffn2_norm, ffn2_w_gate, ffn2_w_up, ffn2_w_down, ple_norm, w_ple_gate, w_ple_proj, final_norm):
    batch, seq, d_model = x.shape
    depth = ffn1_norm.shape[0]
    assert depth == 1, "the final norm is fused into the last layer's embedding step"
    assert seq // SLC_BLOCK <= LANES and seq % 512 == 0
    n = batch * seq
    h = x.reshape(n, d_model)
    tabs = _rope_tables(positions, tm=1024)
    i = 0
    small = lambda a: a[i].astype(bf16)
    rows = lambda a: _cast_bf16(a[i], bk=min(FFN_TILE, a.shape[1]), bn=a.shape[2])
    tiles = lambda a: _cast_bf16(a[i], bk=a.shape[1], bn=FFN_TILE, tiled=True)
    w = dict(
        d_ff=ffn1_w_gate.shape[2], ffn1_norm=ffn1_norm[i], ffn1_w_gate=tiles(ffn1_w_gate), ffn1_w_up=tiles(ffn1_w_up),
        ffn1_w_down=rows(ffn1_w_down), mix_norm=mix_norm[i], w_in_split=_split_w_in(w_in[i]),
        sinks=sinks[i], nsa_pe_k=nsa_pe_k[i], nsa_w_ck1=small(nsa_w_ck1), nsa_w_ck2=small(nsa_w_ck2),
        nsa_pe_v=nsa_pe_v[i], nsa_w_cv1=small(nsa_w_cv1), nsa_w_cv2=small(nsa_w_cv2),
        w_o_a=rows(w_o_a), w_o_b=rows(w_o_b), w_o=rows(w_o),
        ffn2_norm=ffn2_norm[i], ffn2_w_gate=tiles(ffn2_w_gate), ffn2_w_up=tiles(ffn2_w_up),
        ffn2_w_down=rows(ffn2_w_down))
    h = _layer(h, tabs, w, batch=batch, seq=seq)
    out = _ple(_norm_cast(h, ple_norm[i], tm=512), h, p[i].reshape(n, -1), rows(w_ple_gate),
               rows(w_ple_proj), final_norm, tm=512, tn=1024)
    return out.reshape(batch, seq, d_model)
```

```python
import functools

import numpy as np
import jax
import jax.numpy as jnp
from jax import lax
from jax.experimental import pallas as pl
from jax.experimental.pallas import tpu as pltpu

f32 = jnp.float32
bf16 = jnp.bfloat16

A_HEADS, A_KV_HEADS, A_HEAD_DIM, A_WINDOW = 32, 4, 64, 128
B_HEADS, B_KV_GROUPS, B_HEAD_DIM = 16, 2, 128
CMP_BLOCK, CMP_STRIDE = 32, 16
SLC_BLOCK, N_SELECT, B_WINDOW = 64, 16, 512
BAND_BLOCK = 128
ROPE_THETA = 10000.0
EPS = 1e-6

LANES = 128
VMEM_LIMIT = 56 * 1024 * 1024
VMEM_LIMIT_BIG = 60 * 1024 * 1024
FFN_TILE = 512
FFN_DOWN_CHUNK = 1024
FFN_NORM_ROWS = 128
MASKED = -1e30
M_INIT = -1e29

_NT = (((1,), (1,)), ((), ()))
_TN = (((0,), (0,)), ((), ()))


def _params(sem, limit=VMEM_LIMIT):
    return pltpu.CompilerParams(dimension_semantics=sem, vmem_limit_bytes=limit)


def _rmsnorm(x, g):
    var = jnp.mean(x * x, axis=-1, keepdims=True)
    return x * lax.rsqrt(var + EPS) * g


def _sigmoid(x):
    return 1.0 / (1.0 + jnp.exp(-x))


def _lane_tile(x, n):
    return x if n == 1 else jnp.concatenate([x] * n, axis=1)


def _gate_row(gate_ref, branch, heads):
    g_t = _sigmoid(gate_ref[...].astype(f32)).T
    return jnp.concatenate([g_t[3 * h + branch:3 * h + branch + 1, :] for h in heads], axis=1)


def _cast_kernel(w_ref, o_ref, *, k, n):
    bk, bn = w_ref.shape
    w = w_ref[...]
    if k % bk or n % bn:
        row = pl.program_id(0) * bk + lax.broadcasted_iota(jnp.int32, (bk, bn), 0)
        col = pl.program_id(1) * bn + lax.broadcasted_iota(jnp.int32, (bk, bn), 1)
        w = jnp.where((row < k) & (col < n), w, 0.0)
    o_ref[...] = w.astype(o_ref.dtype)


def _cast_bf16(w, *, bk, bn, tiled=False):
    k, n = w.shape
    gk, gn = pl.cdiv(k, bk), pl.cdiv(n, bn)
    if tiled:
        out = jax.ShapeDtypeStruct((gn, gk * bk, bn), bf16)
        o_spec = pl.BlockSpec((None, bk, bn), lambda i, j: (j, i, 0))
    else:
        out = jax.ShapeDtypeStruct((gk * bk, gn * bn), bf16)
        o_spec = pl.BlockSpec((bk, bn), lambda i, j: (i, j))
    return pl.pallas_call(
        functools.partial(_cast_kernel, k=k, n=n),
        out_shape=out,
        grid=(gk, gn),
        in_specs=[pl.BlockSpec((bk, bn), lambda i, j: (i, j))],
        out_specs=o_spec,
        compiler_params=_params(("parallel", "parallel")),
        name="cast_bf16",
    )(w)


def _ffn_kernel(x_ref, g_ref, wg_ref, wu_ref, wd_ref, o_ref, xn_ref, *, tail):
    j = pl.program_id(1)

    @pl.when(j == 0)
    def _():
        for c in range(x_ref.shape[0] // FFN_NORM_ROWS):
            rows = slice(c * FFN_NORM_ROWS, (c + 1) * FFN_NORM_ROWS)
            x = x_ref[rows, :]
            xn_ref[rows, :] = _rmsnorm(x, g_ref[...]).astype(bf16)
            o_ref[rows, :] = x

    def step(width):
        xn = xn_ref[...]
        gate = jnp.dot(xn, wg_ref[:, :width], preferred_element_type=f32)
        up = jnp.dot(xn, wu_ref[:, :width], preferred_element_type=f32)
        act = (0.5 * gate * _sigmoid(gate) * up).astype(bf16)
        d = o_ref.shape[1]
        for c in range(d // FFN_DOWN_CHUNK):
            cols = slice(c * FFN_DOWN_CHUNK, (c + 1) * FFN_DOWN_CHUNK)
            o_ref[:, cols] += jnp.dot(act, wd_ref[:width, cols], preferred_element_type=f32)

    tf = wg_ref.shape[1]
    if tail == tf:
        step(tf)
    else:
        last = pl.num_programs(1) - 1
        pl.when(j != last)(lambda: step(tf))
        pl.when(j == last)(lambda: step(tail))


def _ffn(x, g, wg, wu, wd, *, tm, ff):
    n, d = x.shape
    n_tiles, _, tf = wg.shape
    once = dict(pipeline_mode=pl.Buffered(1))
    return pl.pallas_call(
        functools.partial(_ffn_kernel, tail=ff - (n_tiles - 1) * tf),
        out_shape=jax.ShapeDtypeStruct((n, d), f32),
        grid=(n // tm, n_tiles),
        in_specs=[
            pl.BlockSpec((tm, d), lambda i, j: (i, 0), **once),
            pl.BlockSpec((1, d), lambda i, j: (0, 0)),
            pl.BlockSpec((None, d, tf), lambda i, j: (j, 0, 0)),
            pl.BlockSpec((None, d, tf), lambda i, j: (j, 0, 0)),
            pl.BlockSpec((tf, d), lambda i, j: (j, 0)),
        ],
        out_specs=pl.BlockSpec((tm, d), lambda i, j: (i, 0)),
        scratch_shapes=[pltpu.VMEM((tm, d), bf16)],
        compiler_params=_params(("parallel", "arbitrary"), VMEM_LIMIT_BIG),
        name="ffn",
    )(x, g.reshape(1, d), wg, wu, wd)


def _norm_cast_kernel(x_ref, g_ref, o_ref):
    o_ref[...] = _rmsnorm(x_ref[...], g_ref[...]).astype(bf16)


def _norm_cast(x, g, *, tm):
    n, d = x.shape
    return pl.pallas_call(
        _norm_cast_kernel,
        out_shape=jax.ShapeDtypeStruct((n, d), bf16),
        grid=(n // tm,),
        in_specs=[pl.BlockSpec((tm, d), lambda i: (i, 0)),
                  pl.BlockSpec((1, d), lambda i: (0, 0))],
        out_specs=pl.BlockSpec((tm, d), lambda i: (i, 0)),
        compiler_params=_params(("parallel",)),
        name="norm_cast",
    )(x, g.reshape(1, d))


def _rope_table_kernel(pos_ref, inv64_ref, inv128_ref, c64_ref, s64_ref, c128_ref, s128_ref):
    pos = pos_ref[...].astype(f32)
    lane = lax.broadcasted_iota(jnp.int32, c64_ref.shape, 1)
    a64 = pos * inv64_ref[...]
    a128 = pos * inv128_ref[...]
    c64_ref[...] = jnp.cos(a64)
    s64_ref[...] = jnp.where(lane % A_HEAD_DIM < A_HEAD_DIM // 2, -1.0, 1.0) * jnp.sin(a64)
    c128_ref[...] = jnp.cos(a128)
    s128_ref[...] = jnp.where(lane < B_HEAD_DIM // 2, -1.0, 1.0) * jnp.sin(a128)


def _rope_tables(positions, *, tm):
    n = positions.size

    def inv(d):
        v = jnp.power(jnp.float32(ROPE_THETA), -jnp.arange(0, d, 2, dtype=f32) / d)
        return jnp.tile(v, LANES // (d // 2)).reshape(1, LANES)

    tab = jax.ShapeDtypeStruct((n, LANES), f32)
    row = pl.BlockSpec((1, LANES), lambda i: (0, 0))
    blk = pl.BlockSpec((tm, LANES), lambda i: (i, 0))
    return pl.pallas_call(
        _rope_table_kernel,
        out_shape=(tab, tab, tab, tab),
        grid=(n // tm,),
        in_specs=[pl.BlockSpec((tm, 1), lambda i: (i, 0)), row, row],
        out_specs=(blk, blk, blk, blk),
        compiler_params=_params(("parallel",)),
        name="rope_tables",
    )(positions.reshape(n, 1), inv(A_HEAD_DIM), inv(B_HEAD_DIM))


def _proj_plain_kernel(x_ref, w_ref, o_ref):
    o_ref[...] = jnp.dot(x_ref[...], w_ref[...], preferred_element_type=f32).astype(o_ref.dtype)


def _proj_rope64_kernel(x_ref, w_ref, c_ref, s_ref, o_ref):
    acc = jnp.dot(x_ref[...], w_ref[...], preferred_element_type=f32)
    tn = acc.shape[1]
    half = A_HEAD_DIM // 2
    first = lax.broadcasted_iota(jnp.int32, c_ref.shape, 1) % A_HEAD_DIM < half
    cos, sin = c_ref[...], s_ref[...]
    for c in range(tn // LANES):
        seg = acc[:, c * LANES:(c + 1) * LANES]
        rot = jnp.where(first, pltpu.roll(seg, LANES - half, 1), pltpu.roll(seg, half, 1))
        o_ref[:, c * LANES:(c + 1) * LANES] = (seg * cos + rot * sin).astype(o_ref.dtype)


def _proj_rope128_kernel(x_ref, w_ref, c_ref, s_ref, o_rot_ref, o_raw_ref, *, n_scaled_tiles, scale):
    acc = jnp.dot(x_ref[...], w_ref[...], preferred_element_type=f32)
    acc = acc * jnp.where(pl.program_id(1) < n_scaled_tiles, scale, 1.0).astype(f32)
    tn = acc.shape[1]
    cos, sin = c_ref[...], s_ref[...]
    o_raw_ref[...] = acc.astype(o_raw_ref.dtype)
    for c in range(tn // LANES):
        seg = acc[:, c * LANES:(c + 1) * LANES]
        rot = pltpu.roll(seg, B_HEAD_DIM // 2, 1)
        o_rot_ref[:, c * LANES:(c + 1) * LANES] = (seg * cos + rot * sin).astype(o_rot_ref.dtype)


def _proj(xn, w, *, tm, tn, mode, tables=None, n_scaled_tiles=0, scale=1.0):
    n, d = xn.shape
    width = w.shape[1]
    x_spec = pl.BlockSpec((tm, d), lambda i, j: (i, 0))
    w_spec = pl.BlockSpec((d, tn), lambda i, j: (0, j))
    o_spec = pl.BlockSpec((tm, tn), lambda i, j: (i, j))
    t_spec = pl.BlockSpec((tm, LANES), lambda i, j: (i, 0))
    out = jax.ShapeDtypeStruct((n, width), bf16)
    common = dict(grid=(n // tm, width // tn),
                  compiler_params=_params(("parallel", "arbitrary")))
    if mode == "plain":
        return pl.pallas_call(_proj_plain_kernel, out_shape=out, in_specs=[x_spec, w_spec],
                              out_specs=o_spec, name="proj_plain", **common)(xn, w)
    if mode == "rope64":
        return pl.pallas_call(_proj_rope64_kernel, out_shape=out,
                              in_specs=[x_spec, w_spec, t_spec, t_spec],
                              out_specs=o_spec, name="proj_rope64", **common)(xn, w, *tables)
    kern = functools.partial(_proj_rope128_kernel, n_scaled_tiles=n_scaled_tiles, scale=scale)
    return pl.pallas_call(kern, out_shape=(out, out),
                          in_specs=[x_spec, w_spec, t_spec, t_spec],
                          out_specs=(o_spec, o_spec), name="proj_rope128", **common)(xn, w, *tables)


def _banded_kernel(sink_ref, q_ref, k_ref, v_ref, *rest, groups, rep, dim, window, prev,
                   tq, use_sink, q_scale, gate_branch):
    gate_ref, o_ref = rest if gate_branch is not None else (None,) + rest
    q0 = pl.program_id(1) * tq
    slab = prev + tq
    pack = LANES // dim
    cols = rep // pack
    start = pl.multiple_of(jnp.maximum(q0 - prev, 0), tq)
    kpos = start + lax.broadcasted_iota(jnp.int32, (slab, tq), 0)
    tpos = q0 + lax.broadcasted_iota(jnp.int32, (slab, tq), 1)
    diff = tpos - kpos
    bias = _lane_tile(jnp.where((diff >= 0) & (diff < window), 0.0, -jnp.inf), cols)
    zeros = jnp.zeros((slab, LANES - dim), bf16)
    for g in range(groups):
        q = jnp.concatenate([q_ref[:, (g * cols + c) * LANES:(g * cols + c + 1) * LANES]
                             for c in range(cols)], axis=0)
        if q_scale != 1.0:
            q = q * q_scale
        kg = k_ref[pl.ds(start, slab), g * dim:(g + 1) * dim]
        vg = v_ref[pl.ds(start, slab), g * dim:(g + 1) * dim]
        out_t = None
        for sub in range(pack):
            if pack == 1:
                kk, vv = kg, vg
            else:
                parts = lambda a: [a, zeros] if sub == 0 else [zeros, a]
                kk, vv = jnp.concatenate(parts(kg), axis=1), jnp.concatenate(parts(vg), axis=1)
            s = lax.dot_general(kk, q, _NT, preferred_element_type=f32) + bias
            m = jnp.max(s, axis=0, keepdims=True)
            if use_sink:
                sk = jnp.concatenate([jnp.full((1, tq), sink_ref[g * rep + pack * c + sub], f32)
                                      for c in range(cols)], axis=1)
                m = jnp.maximum(m, sk)
            e = jnp.exp(s - m)
            den = jnp.sum(e, axis=0, keepdims=True)
            if use_sink:
                den = den + jnp.exp(sk - m)
            o_t = lax.dot_general(vv, e.astype(bf16), _TN, preferred_element_type=f32)
            scale = 1.0 / den
            if gate_branch is not None:
                scale = scale * _gate_row(gate_ref, gate_branch,
                                          [g * rep + pack * c + sub for c in range(cols)])
            o_t = o_t * scale
            out_t = o_t if out_t is None else out_t + o_t
        for c in range(cols):
            o_ref[:, (g * cols + c) * LANES:(g * cols + c + 1) * LANES] = (
                out_t[:, c * tq:(c + 1) * tq].T.astype(o_ref.dtype))


def _banded(q, k, v, sinks, *, batch, seq, q_col, k_col, v_col, groups, rep, dim, window,
            use_sink, q_scale=1.0, gates=None, gate_col=0, gate_branch=None):
    tq = BAND_BLOCK
    prev = -(-(window - 1) // tq) * tq
    nq = seq // tq
    hw, gw = groups * rep * dim, groups * dim
    assert LANES % dim == 0 and rep % (LANES // dim) == 0
    kern = functools.partial(_banded_kernel, groups=groups, rep=rep, dim=dim, window=window,
                             prev=prev, tq=tq, use_sink=use_sink, q_scale=q_scale,
                             gate_branch=gate_branch)
    in_specs = [
        pl.BlockSpec(memory_space=pltpu.SMEM),
        pl.BlockSpec((tq, hw), lambda b, i: (b * nq + i, q_col)),
        pl.BlockSpec((seq, gw), lambda b, i: (b, k_col)),
        pl.BlockSpec((seq, gw), lambda b, i: (b, v_col)),
    ]
    args = [sinks, q, k, v]
    if gate_branch is not None:
        in_specs.append(pl.BlockSpec((tq, LANES), lambda b, i: (b * nq + i, gate_col)))
        args.append(gates)
    return pl.pallas_call(
        kern,
        out_shape=jax.ShapeDtypeStruct((batch * seq, hw), bf16),
        grid=(batch, nq),
        in_specs=in_specs,
        out_specs=pl.BlockSpec((tq, hw), lambda b, i: (b * nq + i, 0)),
        compiler_params=_params(("parallel", "arbitrary")),
        name="banded_sink" if use_sink else "banded",
    )(*args)


def _gelu_tanh(x):
    return x * (0.5 * (1.0 + jnp.tanh(np.sqrt(2.0 / np.pi).astype(np.float32)
                                      * (x + 0.044715 * (x * x * x)))))


def _compress_kernel(r_ref, pek_ref, wk1_ref, wk2_ref, pev_ref, wv1_ref, wv2_ref,
                     kc_ref, vc_ref, *, groups, dim, n_real):
    ncp = r_ref.shape[0]
    tok_w = 2 * groups * dim
    halves = CMP_BLOCK // CMP_STRIDE
    row = lax.broadcasted_iota(jnp.int32, (ncp, dim), 0)
    for which, (pe_ref, w1_ref, w2_ref, out_ref) in enumerate(
            ((pek_ref, wk1_ref, wk2_ref, kc_ref), (pev_ref, wv1_ref, wv2_ref, vc_ref))):
        for g in range(groups):
            h1 = None
            for half in range(halves):
                pieces = []
                for tl in range(CMP_STRIDE):
                    l = half * CMP_STRIDE + tl
                    col = tl * tok_w + which * groups * dim + g * dim
                    pieces.append((r_ref[:, col:col + dim].astype(f32) + pe_ref[l:l + 1, :]).astype(bf16))
                x = jnp.concatenate(pieces, axis=1)
                w = w1_ref[half * CMP_STRIDE * dim:(half + 1) * CMP_STRIDE * dim, :]
                part = jnp.dot(x, w, preferred_element_type=f32)
                if half:
                    part = pltpu.roll(part, ncp - half, 0)
                h1 = part if h1 is None else h1 + part
            y = _gelu_tanh(h1).astype(bf16)
            out = jnp.dot(y, w2_ref[...], preferred_element_type=f32)
            out_ref[g] = jnp.where(row < n_real, out, 0.0).astype(out_ref.dtype)


def _compress(r, pe_k, wk1, wk2, pe_v, wv1, wv2, *, batch, groups, dim, n_real):
    _, ncp, width = r.shape
    hidden = wk1.shape[1]
    full = lambda shape: pl.BlockSpec(shape, lambda b: (0,) * len(shape))
    out = jax.ShapeDtypeStruct((batch, groups, ncp, dim), bf16)
    o_spec = pl.BlockSpec((None, groups, ncp, dim), lambda b: (b, 0, 0, 0))
    kern = functools.partial(_compress_kernel, groups=groups, dim=dim, n_real=n_real)
    return pl.pallas_call(
        kern,
        out_shape=(out, out),
        grid=(batch,),
        in_specs=[pl.BlockSpec((None, ncp, width), lambda b: (b, 0, 0)),
                  full((CMP_BLOCK, dim)), full((CMP_BLOCK * dim, hidden)), full((hidden, dim)),
                  full((CMP_BLOCK, dim)), full((CMP_BLOCK * dim, hidden)), full((hidden, dim))],
        out_specs=(o_spec, o_spec),
        compiler_params=_params(("parallel",)),
        name="compress",
    )(r, pe_k, wk1, wk2, pe_v, wv1, wv2)


def _cmp_kernel(q_ref, kc_ref, vc_ref, selt_ref, gate_ref, o_ref, mb_ref, imp_ref, *, groups, rep, dim, tq,
                n_sel, chunk):
    q0 = pl.program_id(2) * tq
    ncp = kc_ref.shape[0]
    nsp = selt_ref.shape[0]
    q = jnp.concatenate([q_ref[:, r * dim:(r + 1) * dim] for r in range(rep)], axis=0)

    gate = _gate_row(gate_ref, 0, range(rep))
    for g in range(1, groups):
        gate = jnp.where(pl.program_id(1) == g, _gate_row(gate_ref, 0, range(g * rep, (g + 1) * rep)), gate)

    def attend(nk):
        cidx = lax.broadcasted_iota(jnp.int32, (nk, tq), 0)
        tpos = q0 + lax.broadcasted_iota(jnp.int32, (nk, tq), 1)
        bias = _lane_tile(jnp.where(cidx * CMP_STRIDE + (CMP_BLOCK - 1) <= tpos, 0.0, -jnp.inf), rep)
        s = lax.dot_general(kc_ref[:nk, :], q, _NT, preferred_element_type=f32) + bias
        m = jnp.max(s, axis=0, keepdims=True)
        m = jnp.where(m > -jnp.inf, m, 0.0)
        e = jnp.exp(s - m)
        den = jnp.sum(e, axis=0, keepdims=True)
        p = e * (1.0 / jnp.where(den > 0, den, 1.0))
        o_t = lax.dot_general(vc_ref[:nk, :], p.astype(bf16), _TN, preferred_element_type=f32)
        o_t = o_t * gate
        psum = None
        for r in range(rep):
            o_ref[:, r * dim:(r + 1) * dim] = o_t[:, r * tq:(r + 1) * tq].T.astype(o_ref.dtype)
            pr = p[:, r * tq:(r + 1) * tq]
            psum = pr if psum is None else psum + pr
        imp_ref[...] = jnp.dot(selt_ref[:, :nk], psum, preferred_element_type=f32,
                               precision=lax.Precision.HIGHEST)

    n_vis = (q0 + tq - CMP_BLOCK) // CMP_STRIDE + 1
    n_chunks = ncp // chunk
    needed = jnp.minimum((n_vis + chunk - 1) // chunk, n_chunks)
    for k in range(1, n_chunks + 1):
        pl.when(jnp.maximum(needed, 1) == k)(functools.partial(attend, k * chunk))

    imp = imp_ref[...]
    j = lax.broadcasted_iota(jnp.int32, (nsp, tq), 0)
    tt = q0 + lax.broadcasted_iota(jnp.int32, (nsp, tq), 1)
    cur = tt // SLC_BLOCK
    forced = (j == 0) | (j == cur) | (j == cur - 1)
    valid = j * SLC_BLOCK <= tt
    score = jnp.where(forced, jnp.inf, jnp.where(valid, imp, -jnp.inf))
    chosen = jnp.zeros((nsp, tq), f32)
    jf = j.astype(f32)
    for _ in range(n_sel):
        best = jnp.max(score, axis=0, keepdims=True)
        first = jnp.min(jnp.where(score == best, jf, float(nsp)), axis=0, keepdims=True)
        hit = jf == first
        chosen = jnp.where(hit, 1.0, chosen)
        score = jnp.where(hit, -jnp.inf, score)
    mb_ref[...] = jnp.where(chosen > 0, 0.0, MASKED).T.astype(mb_ref.dtype)


def _cmp_attention(q_raw, kcmp, vcmp, sel_t, gates, *, batch, seq, groups, rep, dim, tq, n_sel, gate_col):
    nq = seq // tq
    ncp = kcmp.shape[2]
    nsp = sel_t.shape[0]
    chunk = min(LANES, ncp)
    kern = functools.partial(_cmp_kernel, groups=groups, rep=rep, dim=dim, tq=tq, n_sel=n_sel, chunk=chunk)
    kv_spec = pl.BlockSpec((None, None, ncp, dim), lambda b, g, i: (b, g, 0, 0))
    return pl.pallas_call(
        kern,
        out_shape=(jax.ShapeDtypeStruct((batch * seq, groups * rep * dim), bf16),
                   jax.ShapeDtypeStruct((batch, groups, seq, nsp), bf16)),
        grid=(batch, groups, nq),
        in_specs=[pl.BlockSpec((tq, rep * dim), lambda b, g, i: (b * nq + i, g)),
                  kv_spec, kv_spec,
                  pl.BlockSpec((nsp, ncp), lambda b, g, i: (0, 0)),
                  pl.BlockSpec((tq, LANES), lambda b, g, i: (b * nq + i, gate_col))],
        out_specs=(pl.BlockSpec((tq, rep * dim), lambda b, g, i: (b * nq + i, g)),
                   pl.BlockSpec((None, None, tq, nsp), lambda b, g, i: (b, g, i, 0))),
        scratch_shapes=[pltpu.VMEM((nsp, tq), f32)],
        compiler_params=_params(("parallel", "parallel", "arbitrary")),
        name="cmp_attention",
    )(q_raw, kcmp, vcmp, sel_t, gates)


def _slc_kernel(q_ref, mb_ref, k_ref, v_ref, gate_ref, o_ref, kp_ref, qp_ref, *, groups, rep, dim, tq, kt):
    i = pl.program_id(1)
    seq = k_ref.shape[0]
    nsp = mb_ref.shape[-1]
    rows = rep * tq

    @pl.when(i == 0)
    def _():
        blk = lax.broadcasted_iota(jnp.int32, (seq, nsp), 0) // SLC_BLOCK
        lane = lax.broadcasted_iota(jnp.int32, (seq, nsp), 1)
        onehot = jnp.where(blk == lane, 1.0, 0.0).astype(bf16)
        for g in range(groups):
            kp_ref[g, :, :dim] = k_ref[:, g * dim:(g + 1) * dim]
            kp_ref[g, :, dim:] = onehot

    for g in range(groups):
        mb = mb_ref[g]
        for r in range(rep):
            h = g * rep + r
            qp_ref[g, r * tq:(r + 1) * tq, :dim] = q_ref[:, h * dim:(h + 1) * dim]
            qp_ref[g, r * tq:(r + 1) * tq, dim:] = mb

    def update(g, carry, k0, bias):
        m, l, acc = carry
        s = lax.dot_general(kp_ref[g, pl.ds(k0, kt), :], qp_ref[g], _NT, preferred_element_type=f32)
        if bias is not None:
            s = s + bias
        m_new = jnp.maximum(m, jnp.max(s, axis=0, keepdims=True))
        alpha = jnp.exp(m - m_new)
        p = jnp.exp(s - m_new)
        l = alpha * l + jnp.sum(p, axis=0, keepdims=True)
        acc = alpha * acc + lax.dot_general(v_ref[pl.ds(k0, kt), g * dim:(g + 1) * dim], p.astype(bf16),
                                            _TN, preferred_element_type=f32)
        return m_new, l, acc

    def body(t, carries):
        k0 = pl.multiple_of(t * kt, kt)
        return tuple(update(g, carries[g], k0, None) for g in range(groups))

    n_full = (i * tq) // kt
    init = (jnp.full((1, rows), M_INIT, f32), jnp.zeros((1, rows), f32), jnp.zeros((dim, rows), f32))
    carries = lax.fori_loop(0, n_full, body, (init,) * groups)
    k0 = pl.multiple_of(n_full * kt, kt)
    kpos = k0 + lax.broadcasted_iota(jnp.int32, (kt, tq), 0)
    tpos = i * tq + lax.broadcasted_iota(jnp.int32, (kt, tq), 1)
    causal = _lane_tile(jnp.where(kpos <= tpos, 0.0, MASKED), rep)
    for g in range(groups):
        _, l, acc = update(g, carries[g], k0, causal)
        o_t = acc * (_gate_row(gate_ref, 1, range(g * rep, (g + 1) * rep)) / l)
        for r in range(rep):
            h = g * rep + r
            o_ref[:, h * dim:(h + 1) * dim] = o_t[:, r * tq:(r + 1) * tq].T.astype(o_ref.dtype)


def _slc_attention(q_rot, mbias, k_rot, v, gates, *, batch, seq, groups, rep, dim, tq, kt,
                   k_col, v_col, gate_col):
    nq = seq // tq
    nsp = mbias.shape[-1]
    hw, gw = groups * rep * dim, groups * dim
    kern = functools.partial(_slc_kernel, groups=groups, rep=rep, dim=dim, tq=tq, kt=kt)
    return pl.pallas_call(
        kern,
        out_shape=jax.ShapeDtypeStruct((batch * seq, hw), bf16),
        grid=(batch, nq),
        in_specs=[pl.BlockSpec((tq, hw), lambda b, i: (b * nq + i, 0)),
                  pl.BlockSpec((None, groups, tq, nsp), lambda b, i: (b, 0, i, 0)),
                  pl.BlockSpec((seq, gw), lambda b, i: (b, k_col)),
                  pl.BlockSpec((seq, gw), lambda b, i: (b, v_col)),
                  pl.BlockSpec((tq, LANES), lambda b, i: (b * nq + i, gate_col))],
        out_specs=pl.BlockSpec((tq, hw), lambda b, i: (b * nq + i, 0)),
        scratch_shapes=[pltpu.VMEM((groups, seq, dim + nsp), bf16),
                        pltpu.VMEM((groups, rep * tq, dim + nsp), bf16)],
        compiler_params=_params(("parallel", "arbitrary")),
        name="slc_attention",
    )(q_rot, mbias, k_rot, v, gates)


def _merge_kernel(oa_ref, oc_ref, os_ref, ow_ref, ga_ref, gbm_ref, woa_ref, wob_ref, o_ref, ob_ref):
    @pl.when(pl.program_id(1) == 0)
    def _():
        ob = oc_ref[...].astype(f32) + os_ref[...].astype(f32) + ow_ref[...].astype(f32)
        ob_ref[...] = ob.astype(bf16)

    ya = jnp.dot(oa_ref[...], woa_ref[...], preferred_element_type=f32)
    yb = jnp.dot(ob_ref[...], wob_ref[...], preferred_element_type=f32)
    merged = _sigmoid(ga_ref[...].astype(f32)) * ya + _sigmoid(gbm_ref[...].astype(f32)) * yb
    o_ref[...] = merged.astype(o_ref.dtype)


def _merge(o_a, o_cmp, o_slc, o_win, gates, w_oa, w_ob, *, tm, tn, ga_col, gbm_col):
    n, da = o_a.shape
    db = o_cmp.shape[1]
    d = w_oa.shape[1]
    row = lambda w: pl.BlockSpec((tm, w), lambda i, j: (i, 0))
    return pl.pallas_call(
        _merge_kernel,
        out_shape=jax.ShapeDtypeStruct((n, d), bf16),
        grid=(n // tm, d // tn),
        in_specs=[row(da), row(db), row(db), row(db),
                  pl.BlockSpec((tm, tn), lambda i, j: (i, ga_col + j)),
                  pl.BlockSpec((tm, tn), lambda i, j: (i, gbm_col + j)),
                  pl.BlockSpec((da, tn), lambda i, j: (0, j)),
                  pl.BlockSpec((db, tn), lambda i, j: (0, j))],
        out_specs=pl.BlockSpec((tm, tn), lambda i, j: (i, j)),
        scratch_shapes=[pltpu.VMEM((tm, db), bf16)],
        compiler_params=_params(("parallel", "arbitrary")),
        name="merge",
    )(o_a, o_cmp, o_slc, o_win, gates, gates, w_oa, w_ob)


def _matmul_res_kernel(x_ref, w_ref, r_ref, o_ref):
    o_ref[...] = r_ref[...] + jnp.dot(x_ref[...], w_ref[...], preferred_element_type=f32)


def _matmul_res(x, w, res, *, tm, tn):
    n, k = x.shape
    d = w.shape[1]
    return pl.pallas_call(
        _matmul_res_kernel,
        out_shape=jax.ShapeDtypeStruct((n, d), f32),
        grid=(n // tm, d // tn),
        in_specs=[pl.BlockSpec((tm, k), lambda i, j: (i, 0)),
                  pl.BlockSpec((k, tn), lambda i, j: (0, j)),
                  pl.BlockSpec((tm, tn), lambda i, j: (i, j))],
        out_specs=pl.BlockSpec((tm, tn), lambda i, j: (i, j)),
        compiler_params=_params(("parallel", "arbitrary")),
        name="matmul_res",
    )(x, w, res)


def _ple_kernel(xn_ref, h_ref, p_ref, wg_ref, wp_ref, fn_ref, o_ref, h4_ref):
    j = pl.program_id(1)
    nj = pl.num_programs(1)
    tn = h_ref.shape[1]
    gate = _sigmoid(jnp.dot(xn_ref[...], wg_ref[...], preferred_element_type=f32))
    emb = jnp.dot(p_ref[...].astype(bf16), wp_ref[...], preferred_element_type=f32)
    h4_ref[j] = h_ref[...] + emb * gate

    @pl.when(j == nj - 1)
    def _():
        n_tiles = h4_ref.shape[0]
        ss = None
        for c in range(n_tiles):
            t = h4_ref[c]
            part = jnp.sum(t * t, axis=-1, keepdims=True)
            ss = part if ss is None else ss + part
        inv = lax.rsqrt(ss / (n_tiles * tn) + EPS)
        for c in range(n_tiles):
            o_ref[:, c * tn:(c + 1) * tn] = h4_ref[c] * inv * fn_ref[:, c * tn:(c + 1) * tn]


def _ple(xn, h, p, w_gate, w_proj, final_norm, *, tm, tn):
    n, d = h.shape
    pd = p.shape[1]
    return pl.pallas_call(
        _ple_kernel,
        out_shape=jax.ShapeDtypeStruct((n, d), f32),
        grid=(n // tm, d // tn),
        in_specs=[pl.BlockSpec((tm, d), lambda i, j: (i, 0)),
                  pl.BlockSpec((tm, tn), lambda i, j: (i, j)),
                  pl.BlockSpec((tm, pd), lambda i, j: (i, 0)),
                  pl.BlockSpec((d, tn), lambda i, j: (0, j)),
                  pl.BlockSpec((pd, tn), lambda i, j: (0, j)),
                  pl.BlockSpec((1, d), lambda i, j: (0, 0))],
        out_specs=pl.BlockSpec((tm, d), lambda i, j: (i, 0)),
        scratch_shapes=[pltpu.VMEM((d // tn, tm, tn), f32)],
        compiler_params=_params(("parallel", "arbitrary")),
        name="ple_final",
    )(xn, h, p, w_gate, w_proj, final_norm.reshape(1, d))


def _selection_map_t(nc, ns, ncp, nsp):
    a, b = SLC_BLOCK // CMP_STRIDE, CMP_BLOCK // CMP_STRIDE
    j = np.arange(ns)[:, None, None]
    c = a * j - np.arange(a)[None, :, None] - np.arange(b)[None, None, :]
    jj = np.broadcast_to(j, c.shape)
    ok = (c >= 0) & (c < nc)
    mat = np.zeros((nsp, ncp), np.float32)
    np.add.at(mat, (jj[ok], c[ok]), 1.0)
    return jnp.asarray(mat)


def _regroup_kernel(ia_ref, w_ref, o_ref):
    del ia_ref
    o_ref[...] = w_ref[...].T.astype(o_ref.dtype)


def _regroup_cast(w_t, blocks, *, bn):
    k = w_t.shape[1]
    nb = len(blocks)
    return pl.pallas_call(
        _regroup_kernel,
        out_shape=jax.ShapeDtypeStruct((k, nb * bn), bf16),
        grid_spec=pltpu.PrefetchScalarGridSpec(
            num_scalar_prefetch=1,
            grid=(nb,),
            in_specs=[pl.BlockSpec((bn, k), lambda j, ia: (ia[j], 0))],
            out_specs=pl.BlockSpec((k, bn), lambda j, ia: (0, j))),
        compiler_params=_params(("arbitrary",)),
        name="regroup_cast",
    )(jnp.asarray(blocks, jnp.int32), w_t)


def _shift_cast_kernel(w_ref, o_ref, prev_ref, *, shift):
    @pl.when(pl.program_id(0) > 0)
    def _():
        rows = jnp.concatenate([prev_ref[shift:, :], w_ref[:shift, :]], axis=0)
        o_ref[...] = rows.T.astype(o_ref.dtype)

    prev_ref[...] = w_ref[...]


def _shift_cast(w_t, *, first_block, n_blocks, shift, bn):
    k = w_t.shape[1]
    return pl.pallas_call(
        functools.partial(_shift_cast_kernel, shift=shift),
        out_shape=jax.ShapeDtypeStruct((k, n_blocks * bn), bf16),
        grid=(n_blocks + 1,),
        in_specs=[pl.BlockSpec((bn, k), lambda j: (first_block + j, 0))],
        out_specs=pl.BlockSpec((k, bn), lambda j: (0, jnp.maximum(j - 1, 0))),
        scratch_shapes=[pltpu.VMEM((bn, k), f32)],
        compiler_params=_params(("arbitrary",)),
        name="shift_cast",
    )(w_t)


def _split_w_in(w_in):
    bn = A_KV_HEADS * A_HEAD_DIM
    assert bn == B_KV_GROUPS * B_HEAD_DIM
    d_model = w_in.shape[0]
    qa_b, qb_b, gate_b = A_HEADS * A_HEAD_DIM // bn, B_HEADS * B_HEAD_DIM // bn, d_model // bn
    qa0 = 0
    ka0 = qa0 + qa_b
    va0 = ka0 + 1
    qb0 = va0 + 1
    kc0 = qb0 + qb_b
    vc0, ksl0, vsl0, kwn0, vwn0, gb0 = (kc0 + t for t in range(1, 7))
    w_t = w_in.T
    w_a = _regroup_cast(w_t, list(range(qa0, ka0 + 1)), bn=bn)
    w_b = _regroup_cast(w_t, list(range(qb0, kc0)) + [ksl0, kwn0], bn=bn)
    w_c = _regroup_cast(w_t, [va0, kc0, vc0, vsl0, vwn0, gb0], bn=bn)
    w_g = _shift_cast(w_t, first_block=gb0, n_blocks=2 * gate_b, shift=B_HEADS * 3, bn=bn)
    return w_a, w_b, w_c, w_g


def _layer(h, tabs, w, *, batch, seq):
    n, d_model = h.shape
    c64, s64, c128, s128 = tabs
    kv_a = A_KV_HEADS * A_HEAD_DIM
    kv_b = B_KV_GROUPS * B_HEAD_DIM
    qb_w = B_HEADS * B_HEAD_DIM
    rep_b = B_HEADS // B_KV_GROUPS

    h = _ffn(h, w["ffn1_norm"], w["ffn1_w_gate"], w["ffn1_w_up"], w["ffn1_w_down"], tm=512, ff=w["d_ff"])

    xn = _norm_cast(h, w["mix_norm"], tm=512)
    w_a, w_b, w_c, w_g = w["w_in_split"]
    proj_a = _proj(xn, w_a, tm=1024, tn=768, mode="rope64", tables=(c64, s64))
    proj_b_rot, proj_b_raw = _proj(xn, w_b, tm=1024, tn=512, mode="rope128", tables=(c128, s128),
                                   n_scaled_tiles=qb_w // 512, scale=B_HEAD_DIM ** -0.5)
    c0 = 0
    proj_c = _proj(xn, w_c, tm=1024, tn=512, mode="plain")
    proj_g = _proj(xn, w_g, tm=1024, tn=512, mode="plain")

    o_a = _banded(proj_a, proj_a, proj_c, w["sinks"], batch=batch, seq=seq,
                  q_col=0, k_col=(A_HEADS * A_HEAD_DIM) // kv_a, v_col=c0 // kv_a,
                  groups=A_KV_HEADS, rep=A_HEADS // A_KV_HEADS, dim=A_HEAD_DIM,
                  window=A_WINDOW, use_sink=True, q_scale=A_HEAD_DIM ** -0.5)

    ncp = seq // CMP_STRIDE
    nc = (seq - CMP_BLOCK) // CMP_STRIDE + 1
    ns = seq // SLC_BLOCK
    kvc = proj_c[:, c0 + kv_a:c0 + kv_a + 2 * kv_b].reshape(batch, ncp, CMP_STRIDE * 2 * kv_b)
    kcmp, vcmp = _compress(kvc, w["nsa_pe_k"], w["nsa_w_ck1"], w["nsa_w_ck2"],
                           w["nsa_pe_v"], w["nsa_w_cv1"], w["nsa_w_cv2"],
                           batch=batch, groups=B_KV_GROUPS, dim=B_HEAD_DIM, n_real=nc)
    sel_t = _selection_map_t(nc, ns, ncp, LANES)
    gate_col = (c0 + kv_a + 4 * kv_b) // LANES
    o_cmp, mbias = _cmp_attention(proj_b_raw, kcmp, vcmp, sel_t, proj_c, batch=batch, seq=seq,
                                  groups=B_KV_GROUPS, rep=rep_b, dim=B_HEAD_DIM, tq=128,
                                  n_sel=min(N_SELECT, ns), gate_col=gate_col)
    o_slc = _slc_attention(proj_b_rot, mbias, proj_b_rot, proj_c, proj_c, batch=batch, seq=seq,
                           groups=B_KV_GROUPS, rep=rep_b, dim=B_HEAD_DIM, tq=256,
                           kt=min(1024, seq),
                           k_col=qb_w // kv_b, v_col=(c0 + kv_a + 2 * kv_b) // kv_b, gate_col=gate_col)
    o_win = _banded(proj_b_rot, proj_b_rot, proj_c, w["sinks"], batch=batch, seq=seq,
                    q_col=0, k_col=(qb_w + kv_b) // kv_b, v_col=(c0 + kv_a + 3 * kv_b) // kv_b,
                    groups=B_KV_GROUPS, rep=rep_b, dim=B_HEAD_DIM, window=B_WINDOW,
                    use_sink=False, gates=proj_c, gate_col=gate_col, gate_branch=2)

    tn = 1024
    merged = _merge(o_a, o_cmp, o_slc, o_win, proj_g, w["w_o_a"], w["w_o_b"], tm=512, tn=tn,
                    ga_col=0, gbm_col=d_model // tn)
    h = _matmul_res(merged, w["w_o"], h, tm=1024, tn=512)

    h = _ffn(h, w["ffn2_norm"], w["ffn2_w_gate"], w["ffn2_w_up"], w["ffn2_w_down"], tm=512, ff=w["d_ff"])
    return h


def kernel(x, p, positions, ffn1_norm, ffn1_w_gate, ffn1_w_up, ffn1_w_down, mix_norm, w_in, sinks, nsa_pe_k, nsa_w_ck1, nsa_w_ck2, nsa_pe_v, nsa_w_cv1, nsa_w_cv2, w_o_a, w_o_b, w_o, ffn2_norm, ffn2_w_gate, ffn2_w_up, ffn2_w_down, ple_norm, w_ple_gate, w_ple_proj, final_norm):
    batch, seq, d_model = x.shape
    depth = ffn1_norm.shape[0]
    assert depth == 1, "the final norm is fused into the last layer's embedding step"
    assert seq // SLC_BLOCK <= LANES and seq % 512 == 0
    n = batch * seq
    h = x.reshape(n, d_model)
    tabs = _rope_tables(positions, tm=1024)
    i = 0
    small = lambda a: a[i].astype(bf16)
    rows = lambda a: _cast_bf16(a[i], bk=min(FFN_TILE, a.shape[1]), bn=a.shape[2])
    tiles = lambda a: _cast_bf16(a[i], bk=a.shape[1], bn=FFN_TILE, tiled=True)
    w = dict(
        d_ff=ffn1_w_gate.shape[2], ffn1_norm=ffn1_norm[i], ffn1_w_gate=tiles(ffn1_w_gate), ffn1_w_up=tiles(ffn1_w_up),
        ffn1_w_down=rows(ffn1_w_down), mix_norm=mix_norm[i], w_in_split=_split_w_in(w_in[i]),
        sinks=sinks[i], nsa_pe_k=nsa_pe_k[i], nsa_w_ck1=small(nsa_w_ck1), nsa_w_ck2=small(nsa_w_ck2),
        nsa_pe_v=nsa_pe_v[i], nsa_w_cv1=small(nsa_w_cv1), nsa_w_cv2=small(nsa_w_cv2),
        w_o_a=rows(w_o_a), w_o_b=rows(w_o_b), w_o=rows(w_o),
        ffn2_norm=ffn2_norm[i], ffn2_w_gate=tiles(ffn2_w_gate), ffn2_w_up=tiles(ffn2_w_up),
        ffn2_w_down=rows(ffn2_w_down))
    h = _layer(h, tabs, w, batch=batch, seq=seq)
    out = _ple(_norm_cast(h, ple_norm[i], tm=512), h, p[i].reshape(n, -1), rows(w_ple_gate),
               rows(w_ple_proj), final_norm, tm=512, tn=512)
    return out.reshape(batch, seq, d_model)
```
